```python
import math
import jax
import jax.numpy as jnp
from jax import lax
import numpy as np

D_MODEL = 1024
BATCH = 8
SEQ = 2048
DEPTH = 1
DEC_BATCH = 32
DEC_SEQ = 32
PAST_LEN = 4096

CHUNK = 64
N_META = 16
D_CONV = 512
CONV_WIDTH = 31
N_HEADS = 8
N_KV_HEADS = 2
HEAD_DIM = 64
D_ATTN = N_HEADS * HEAD_DIM
D_KV = N_KV_HEADS * HEAD_DIM
WINDOW = 128
WIN_CHUNKS = WINDOW // CHUNK
N_BUCKETS = 32
MAX_DISTANCE = 256
D_FF = 2816
FFN_CONV_WIDTH = 3
D_IN = 2 * D_CONV + D_ATTN + 2 * D_KV
EPS = 1e-6
NEG_INF = -1e30

kernel_name = 'hymba_conformer_swa_sink_stream_step'


def rmsnorm(x, g):
    xf = x.astype(jnp.float32)
    r = lax.rsqrt(jnp.mean(xf * xf, axis=-1, keepdims=True) + EPS)
    return (xf * r).astype(x.dtype) * g


def layernorm(x, g, b):
    xf = x.astype(jnp.float32)
    mu = jnp.mean(xf, axis=-1, keepdims=True)
    var = jnp.mean(jnp.square(xf - mu), axis=-1, keepdims=True)
    return ((xf - mu) * lax.rsqrt(var + EPS)).astype(x.dtype) * g + b


def depthwise_causal_conv(x_ext, w, b):
    c = x_ext.shape[-1]
    y = lax.conv_general_dilated(x_ext, w[:, None, :], window_strides=(1,), padding='VALID',
                                 dimension_numbers=('NWC', 'WIO', 'NWC'), feature_group_count=c)
    return y + b


def rel_bucket(rel):
    nb = N_BUCKETS // 2
    max_exact = nb // 2
    ret = jnp.where(rel > 0, nb, 0)
    n = jnp.abs(rel)
    nf = jnp.maximum(n, 1).astype(jnp.float32)
    large = max_exact + (jnp.log(nf / max_exact) / math.log(MAX_DISTANCE / max_exact)
                         * (nb - max_exact)).astype(jnp.int32)
    large = jnp.minimum(large, nb - 1)
    return ret + jnp.where(n < max_exact, n, large)


def rel_bias(q_pos, k_pos, table):
    bkt = rel_bucket(k_pos[..., None, :] - q_pos[..., :, None])
    return jnp.moveaxis(table[bkt], -1, -3)


def sink_attention(q, k, v, bias, sinks):
    b_, n_, nq = q.shape[:3]
    g = N_HEADS // N_KV_HEADS
    qg = q.reshape(b_, n_, nq, N_KV_HEADS, g, HEAD_DIM)
    s = jnp.einsum('bnqhgd,bnkhd->bnhgqk', qg, k, preferred_element_type=jnp.float32) * (HEAD_DIM ** -0.5)
    s = s + bias.reshape(n_, N_KV_HEADS, g, nq, -1).astype(jnp.float32)
    sk = sinks.astype(jnp.float32).reshape(1, 1, N_KV_HEADS, g, 1, 1)
    m = jnp.maximum(jnp.max(s, axis=-1, keepdims=True), sk)
    p = jnp.exp(s - m)
    denom = jnp.sum(p, axis=-1, keepdims=True) + jnp.exp(sk - m)
    o = jnp.einsum('bnhgqk,bnkhd->bnqhgd', (p / denom).astype(v.dtype), v)
    return o.reshape(b_, n_, nq, D_ATTN)


def project(h, w_in):
    z = h @ w_in
    a_in, q, k, v = jnp.split(z, [2 * D_CONV, 2 * D_CONV + D_ATTN, 2 * D_CONV + D_ATTN + D_KV], axis=-1)
    glu = a_in[..., :D_CONV] * jax.nn.sigmoid(a_in[..., D_CONV:])
    sh = z.shape[:-1]
    return (glu, q.reshape(*sh, N_HEADS, HEAD_DIM), k.reshape(*sh, N_KV_HEADS, HEAD_DIM),
            v.reshape(*sh, N_KV_HEADS, HEAD_DIM))


def conv_module(glu_ext, dw_w, dw_b, ln_g, ln_b):
    y = depthwise_causal_conv(glu_ext, dw_w, dw_b)
    return jax.nn.silu(layernorm(y, ln_g, ln_b))


def conv_ffn(x, gate_past, g, w_up, dw_w, dw_b, w_down):
    h = rmsnorm(x, g)
    u = h @ w_up
    gate, up = u[..., :D_FF], u[..., D_FF:]
    gate_ext = jnp.concatenate([gate_past, gate], axis=1)
    y = jax.nn.silu(depthwise_causal_conv(gate_ext, dw_w, dw_b)) * up
    return x + y @ w_down, gate_ext[:, -(FFN_CONV_WIDTH - 1):]


def prompt_attention(q, k, v, table, sinks):
    bsz, t = q.shape[:2]
    nc = (t - N_META) // CHUNK
    nband = (WIN_CHUNKS + 1) * CHUNK

    def band(x):
        xc = x[:, N_META:].reshape(bsz, nc, CHUNK, N_KV_HEADS, HEAD_DIM)
        xpad = jnp.pad(xc, ((0, 0), (WIN_CHUNKS, 0), (0, 0), (0, 0), (0, 0)))
        xb = jnp.concatenate([xpad[:, j:j + nc] for j in range(WIN_CHUNKS + 1)], axis=2)
        xm = jnp.broadcast_to(x[:, None, :N_META], (bsz, nc, N_META, N_KV_HEADS, HEAD_DIM))
        return jnp.concatenate([xm, xb], axis=2)

    kb, vb = band(k), band(v)
    meta_pos = jnp.arange(N_META, dtype=jnp.int32)
    cidx = jnp.arange(nc, dtype=jnp.int32)[:, None]
    q_pos = N_META + cidx * CHUNK + jnp.arange(CHUNK, dtype=jnp.int32)[None]
    band_pos = N_META + (cidx - WIN_CHUNKS) * CHUNK + jnp.arange(nband, dtype=jnp.int32)[None]
    k_pos = jnp.concatenate([jnp.broadcast_to(meta_pos[None], (nc, N_META)), band_pos], axis=1)
    valid = jnp.concatenate([jnp.ones((nc, N_META), dtype=bool), band_pos >= N_META], axis=1)
    bias = jnp.where(valid[:, None, None, :], rel_bias(q_pos, k_pos, table), NEG_INF)
    qf = q[:, N_META:].reshape(bsz, nc, CHUNK, N_HEADS, HEAD_DIM)
    o_frames = sink_attention(qf, kb, vb, bias, sinks).reshape(bsz, nc * CHUNK, D_ATTN)
    o_meta = sink_attention(q[:, None, :N_META], k[:, None, :N_META], v[:, None, :N_META],
                            rel_bias(meta_pos[None], meta_pos[None], table), sinks)[:, 0]
    return jnp.concatenate([o_meta, o_frames], axis=1)


def sample_attention(q, k, v, k_meta, v_meta, k_win, v_win, table, sinks):
    n = q.shape[1]
    w = k_win.shape[1]
    k_all = jnp.concatenate([k_meta, k_win, k], axis=1)
    v_all = jnp.concatenate([v_meta, v_win, v], axis=1)
    q_pos = N_META + PAST_LEN + jnp.arange(n, dtype=jnp.int32)
    k_pos = jnp.concatenate([jnp.arange(N_META, dtype=jnp.int32),
                             N_META + PAST_LEN - w + jnp.arange(w, dtype=jnp.int32), q_pos])
    bias = rel_bias(q_pos[None], k_pos[None], table)
    o = sink_attention(q[:, None], k_all[:, None], v_all[:, None], bias, sinks)[:, 0]
    return o, k_all[:, -w:], v_all[:, -w:]


def setup_inputs(seed: int = 0) -> dict:
    key = jax.random.key(seed)
    ks = jax.random.split(key, 24)

    def nrm(k, shape, s):
        return jax.random.normal(k, shape, jnp.float32) * s

    win = min(WINDOW, PAST_LEN)
    return {
        'x_prompt': nrm(ks[0], (BATCH, SEQ, D_MODEL), 1.0),
        'x_sample': nrm(ks[1], (DEC_BATCH, DEC_SEQ, D_MODEL), 1.0),
        'cache_k_meta': nrm(ks[2], (DEPTH, DEC_BATCH, N_META, N_KV_HEADS, HEAD_DIM), 1.0),
        'cache_v_meta': nrm(ks[3], (DEPTH, DEC_BATCH, N_META, N_KV_HEADS, HEAD_DIM), 1.0),
        'cache_k_win': nrm(ks[4], (DEPTH, DEC_BATCH, win, N_KV_HEADS, HEAD_DIM), 1.0),
        'cache_v_win': nrm(ks[5], (DEPTH, DEC_BATCH, win, N_KV_HEADS, HEAD_DIM), 1.0),
        'state_conv_a': nrm(ks[6], (DEPTH, DEC_BATCH, CONV_WIDTH - 1, D_CONV), 0.5),
        'state_conv_ffn': nrm(ks[7], (DEPTH, DEC_BATCH, FFN_CONV_WIDTH - 1, D_FF), 0.5),
        'meta_tokens': nrm(ks[8], (N_META, D_MODEL), 1.0),
        'rel_bias_table': nrm(ks[9], (N_BUCKETS, N_HEADS), 0.5),
        'norm_mix': 1.0 + nrm(ks[10], (DEPTH, D_MODEL), 0.05),
        'w_in': nrm(ks[11], (DEPTH, D_MODEL, D_IN), D_MODEL ** -0.5),
        'conv_dw_w': nrm(ks[12], (DEPTH, CONV_WIDTH, D_CONV), CONV_WIDTH ** -0.5),
        'conv_dw_b': nrm(ks[13], (DEPTH, D_CONV), 0.01),
        'conv_ln_g': 1.0 + nrm(ks[14], (DEPTH, D_CONV), 0.05),
        'conv_ln_b': nrm(ks[15], (DEPTH, D_CONV), 0.01),
        'attn_sinks': nrm(ks[16], (DEPTH, N_HEADS), 1.0),
        'w_out': nrm(ks[17], (DEPTH, D_CONV + D_ATTN, D_MODEL), (D_CONV + D_ATTN) ** -0.5),
        'norm_ffn': 1.0 + nrm(ks[18], (DEPTH, D_MODEL), 0.05),
        'w_up': nrm(ks[19], (DEPTH, D_MODEL, 2 * D_FF), D_MODEL ** -0.5),
        'ffn_dw_w': nrm(ks[20], (DEPTH, FFN_CONV_WIDTH, D_FF), FFN_CONV_WIDTH ** -0.5),
        'ffn_dw_b': nrm(ks[21], (DEPTH, D_FF), 0.01),
        'w_down': nrm(ks[22], (DEPTH, D_FF, D_MODEL), D_FF ** -0.5),
        'norm_final': 1.0 + nrm(ks[23], (D_MODEL,), 0.05),
    }


def reference(x_prompt, x_sample, cache_k_meta, cache_v_meta, cache_k_win, cache_v_win,
              state_conv_a, state_conv_ffn, meta_tokens, rel_bias_table, norm_mix, w_in,
              conv_dw_w, conv_dw_b, conv_ln_g, conv_ln_b, attn_sinks, w_out, norm_ffn,
              w_up, ffn_dw_w, ffn_dw_b, w_down, norm_final):
    bsz = x_prompt.shape[0]
    xp = jnp.concatenate([jnp.broadcast_to(meta_tokens[None].astype(x_prompt.dtype), (bsz, N_META, D_MODEL)),
                          x_prompt], axis=1)
    xs = x_sample
    km_p, vm_p, kw_p, vw_p, ca_p, cf_p = [], [], [], [], [], []
    kw_s, vw_s, ca_s, cf_s = [], [], [], []
    for l in range(DEPTH):
        glu, q, k, v = project(rmsnorm(xp, norm_mix[l]), w_in[l])
        glu_ext = jnp.concatenate([jnp.zeros((bsz, CONV_WIDTH - 1, D_CONV), glu.dtype), glu], axis=1)
        a_out = conv_module(glu_ext, conv_dw_w[l], conv_dw_b[l], conv_ln_g[l], conv_ln_b[l])
        attn = prompt_attention(q, k, v, rel_bias_table, attn_sinks[l])
        xp = xp + jnp.concatenate([a_out, attn], axis=-1) @ w_out[l]
        xp, ffn_state = conv_ffn(xp, jnp.zeros((bsz, FFN_CONV_WIDTH - 1, D_FF), xp.dtype), norm_ffn[l],
                                 w_up[l], ffn_dw_w[l], ffn_dw_b[l], w_down[l])
        win = cache_k_win.shape[2]
        km_p.append(k[:, :N_META])
        vm_p.append(v[:, :N_META])
        kw_p.append(k[:, -win:])
        vw_p.append(v[:, -win:])
        ca_p.append(glu_ext[:, -(CONV_WIDTH - 1):])
        cf_p.append(ffn_state)
        glu, q, k, v = project(rmsnorm(xs, norm_mix[l]), w_in[l])
        glu_ext = jnp.concatenate([state_conv_a[l], glu], axis=1)
        a_out = conv_module(glu_ext, conv_dw_w[l], conv_dw_b[l], conv_ln_g[l], conv_ln_b[l])
        attn, k_new_win, v_new_win = sample_attention(q, k, v, cache_k_meta[l], cache_v_meta[l],
                                                      cache_k_win[l], cache_v_win[l], rel_bias_table, attn_sinks[l])
        xs = xs + jnp.concatenate([a_out, attn], axis=-1) @ w_out[l]
        xs, ffn_state = conv_ffn(xs, state_conv_ffn[l], norm_ffn[l], w_up[l], ffn_dw_w[l], ffn_dw_b[l], w_down[l])
        kw_s.append(k_new_win)
        vw_s.append(v_new_win)
        ca_s.append(glu_ext[:, -(CONV_WIDTH - 1):])
        cf_s.append(ffn_state)
    y_prompt = rmsnorm(xp, norm_final)[:, N_META:]
    y_sample = rmsnorm(xs, norm_final)
    return (y_prompt, y_sample,
            jnp.stack(km_p), jnp.stack(vm_p), jnp.stack(kw_p), jnp.stack(vw_p), jnp.stack(ca_p), jnp.stack(cf_p),
            jnp.stack(kw_s), jnp.stack(vw_s), jnp.stack(ca_s), jnp.stack(cf_s))
```

```python
import functools
import math

import numpy as np
import jax
import jax.numpy as jnp
from jax import lax
from jax.experimental import pallas as pl
from jax.experimental.pallas import tpu as pltpu

F32 = jnp.float32
BF16 = jnp.bfloat16

D_MODEL = 1024
CHUNK = 64
N_META = 16
D_CONV = 512
CONV_WIDTH = 31
N_HEADS = 8
N_KV_HEADS = 2
HEAD_DIM = 64
D_ATTN = N_HEADS * HEAD_DIM
D_KV = N_KV_HEADS * HEAD_DIM
WINDOW = 128
PAST_LEN = 4096
N_BUCKETS = 32
MAX_DISTANCE = 256
D_FF = 2816
FFN_CONV_WIDTH = 3
D_IN = 2 * D_CONV + D_ATTN + 2 * D_KV
EPS = 1e-6
NEG_INF = -1e30
SCALE = HEAD_DIM ** -0.5

_Q0 = 2 * D_CONV
_K0 = _Q0 + D_ATTN
_V0 = _K0 + D_KV

NKEY = 256
BAND = WINDOW + CHUNK
HALO = 32
FFN_CW = 256
TM = 512
SB = 8
VMEM_LIMIT = 60 * 1024 * 1024


def _np_bucket(rel):
    nb = N_BUCKETS // 2
    max_exact = nb // 2
    ret = np.where(rel > 0, nb, 0)
    n = np.abs(rel)
    nf = np.maximum(n, 1).astype(np.float32)
    large = max_exact + (np.log(nf / np.float32(max_exact)) / np.float32(math.log(MAX_DISTANCE / max_exact))
                         * np.float32(nb - max_exact)).astype(np.int32)
    large = np.minimum(large, nb - 1)
    return (ret + np.where(n < max_exact, n, large)).astype(np.int32)


def _bucket_tables(past_len, dec_seq):
    i = np.arange(CHUNK)[:, None]
    j = np.arange(BAND)[None, :]
    m = np.arange(N_META)[None, :]
    band = _np_bucket(j - WINDOW - i)
    pad = np.full((CHUNK, NKEY - BAND - N_META), -1, np.int32)

    def meta_for(c):
        return _np_bucket(m - (N_META + CHUNK * c + i))

    prompt = []
    for c in range(4):
        b = np.where(c - 2 + j // CHUNK >= 0, band, -1)
        prompt.append(np.concatenate([b, meta_for(c), pad], axis=1))
    assert np.array_equal(meta_for(3), meta_for(4096))
    prompt = np.stack(prompt).astype(np.int32)

    i_s = np.arange(dec_seq)[:, None]
    band_s = np.where(j < WINDOW + dec_seq, _np_bucket(j - WINDOW - i_s), -1)
    meta_s = _np_bucket(m - (N_META + past_len + i_s))
    sample = np.concatenate([band_s, meta_s, pad[:dec_seq]], axis=1)[None].astype(np.int32)

    i_m = np.arange(N_META)[:, None]
    meta_self = np.concatenate([np.full((N_META, BAND), -1, np.int32), _np_bucket(m - i_m), pad[:N_META]],
                               axis=1)[None].astype(np.int32)
    return prompt, sample, meta_self


def _bias_kernel(table_ref, bp_ref, bs_ref, bm_ref, op_ref, os_ref, om_ref):
    for b_ref, o_ref in ((bp_ref, op_ref), (bs_ref, os_ref), (bm_ref, om_ref)):
        nv, r, _ = b_ref.shape
        for v in range(nv):
            bk = b_ref[v]
            for g in range(2):
                for pp in range(2):
                    for hf in range(2):
                        h = 4 * g + 2 * pp + hf
                        acc = jnp.full((r, NKEY), NEG_INF, F32)
                        for b in range(N_BUCKETS):
                            acc = jnp.where(bk == b, table_ref[b, h], acc)
                        o_ref[v, g, r * pp:r * (pp + 1), NKEY * hf:NKEY * (hf + 1)] = acc


def _build_bias(table, bp, bs, bm):
    shapes = [jax.ShapeDtypeStruct((b.shape[0], 2, 2 * b.shape[1], 2 * NKEY), F32) for b in (bp, bs, bm)]
    vm = pl.BlockSpec(memory_space=pltpu.VMEM)
    return pl.pallas_call(
        _bias_kernel,
        out_shape=shapes,
        in_specs=[pl.BlockSpec(memory_space=pltpu.SMEM), vm, vm, vm],
        out_specs=[vm, vm, vm],
        name="rel_bias",
    )(table, bp, bs, bm)


def _dot(a, b):
    return jnp.dot(a, b, preferred_element_type=F32)


def _dot_nt(a, b):
    return lax.dot_general(a, b, (((1,), (1,)), ((), ())), preferred_element_type=F32)


def _rms(x, g):
    ms = jnp.mean(x * x, axis=-1, keepdims=True)
    return (x * lax.rsqrt(ms + EPS)) * g


def _lohi(x):
    lo = lax.broadcasted_iota(jnp.int32, x.shape, 1) < HEAD_DIM
    xr = pltpu.roll(x, HEAD_DIM, axis=1)
    zero = jnp.zeros_like(x)
    return (jnp.where(lo, x, zero).astype(BF16), jnp.where(lo, zero, xr).astype(BF16),
            jnp.where(lo, xr, zero).astype(BF16), jnp.where(lo, zero, x).astype(BF16))


def _attn_block(q, k_parts, v_parts, bias_g, sinks_ref):
    r = q.shape[0]
    outs = []
    for g in range(N_KV_HEADS):
        qg = jnp.concatenate([q[:, 256 * g:256 * g + 128], q[:, 256 * g + 128:256 * g + 256]], axis=0).astype(BF16)
        kk = jnp.concatenate([k_parts[2 * g], k_parts[2 * g + 1]], axis=0)
        vv = jnp.concatenate([v_parts[2 * g], v_parts[2 * g + 1]], axis=0)
        s = _dot_nt(qg, kk) * SCALE + bias_g(g)
        rows = []
        for pp in range(2):
            segs = []
            for hf in range(2):
                sink = sinks_ref[4 * g + 2 * pp + hf]
                seg = s[r * pp:r * (pp + 1), NKEY * hf:NKEY * (hf + 1)]
                mx = jnp.maximum(jnp.max(seg, axis=-1, keepdims=True), sink)
                e = jnp.exp(seg - mx)
                den = jnp.sum(e, axis=-1, keepdims=True) + jnp.exp(sink - mx)
                segs.append((e * (1.0 / den)).astype(BF16))
            rows.append(jnp.concatenate(segs, axis=1))
        p = jnp.concatenate(rows, axis=0)
        o = _dot(p, vv)
        outs.append(jnp.concatenate([o[:r], o[r:]], axis=1))
    return outs


def _in_proj(x_ref, gmix_ref, w_in_ref, hb_ref, z_ref, ns, l, rc):
    g = gmix_ref[...]
    for s in range(ns):
        for r0 in range(0, l, rc):
            hb_ref[s * l + r0:s * l + r0 + rc, :] = _rms(x_ref[s, r0:r0 + rc, :], g).astype(BF16)
    z_ref[...] = _dot(hb_ref[...], w_in_ref[...])


def _conv_branch(z_ref, row0, l, glu_buf, halo, dw_w_ref, dw_b_ref, ln_g_ref, ln_b_ref, mix_ref, rc):
    glu_buf[HALO - 30:HALO, :] = halo
    for r0 in range(0, l, rc):
        a = z_ref[row0 + r0:row0 + r0 + rc, 0:D_CONV]
        b = z_ref[row0 + r0:row0 + r0 + rc, D_CONV:2 * D_CONV]
        glu_buf[HALO + r0:HALO + r0 + rc, :] = a * jax.nn.sigmoid(b)
    bias = dw_b_ref[...]
    ln_g = ln_g_ref[...]
    ln_b = ln_b_ref[...]
    for r0 in range(0, l, rc):
        acc = jnp.broadcast_to(bias, (rc, D_CONV))
        for w in range(CONV_WIDTH):
            off = HALO - (CONV_WIDTH - 1) + r0 + w
            acc = acc + glu_buf[off:off + rc, :] * dw_w_ref[w:w + 1, :]
        mu = jnp.mean(acc, axis=-1, keepdims=True)
        d = acc - mu
        var = jnp.mean(d * d, axis=-1, keepdims=True)
        yn = (d * lax.rsqrt(var + EPS)) * ln_g + ln_b
        mix_ref[row0 + r0:row0 + r0 + rc, 0:D_CONV] = (yn * jax.nn.sigmoid(yn)).astype(BF16)
    return glu_buf[HALO + l - 30:HALO + l, :]


def _out_proj(x_ref, mix_ref, w_out_ref, gffn_ref, x1_ref, hb_ref, ns, l, rc):
    g = gffn_ref[...]
    for s in range(ns):
        for r0 in range(0, l, rc):
            rows = slice(s * l + r0, s * l + r0 + rc)
            x1 = x_ref[s, r0:r0 + rc, :] + _dot(mix_ref[rows, :], w_out_ref[...])
            x1_ref[rows, :] = x1
            hb_ref[rows, :] = _rms(x1, g).astype(BF16)


def _ffn_hidden(hb_ref, w_up_ref, fdw_w_ref, fdw_b_ref, gbuf, ubuf, y_ref, carry_in, carry_out, ns, l, rc):
    for ci, c in enumerate(range(0, D_FF, FFN_CW)):
        gb = gbuf.at[ci % 2]
        ub = ubuf.at[ci % 2]
        cols = slice(c, c + FFN_CW)
        ug = _dot(hb_ref[...], w_up_ref[:, c:c + FFN_CW])
        ub[...] = _dot(hb_ref[...], w_up_ref[:, D_FF + c:D_FF + c + FFN_CW])
        w0 = fdw_w_ref[0:1, cols]
        w1 = fdw_w_ref[1:2, cols]
        w2 = fdw_w_ref[2:3, cols]
        bb = fdw_b_ref[:, cols]
        for s in range(ns):
            gb[s, 6:8, :] = carry_in[s, :, cols]
            gb[s, 8:8 + l, :] = ug[s * l:(s + 1) * l]
        for s in range(ns):
            for r0 in range(0, l, rc):
                cv = (bb + gb[s, 6 + r0:6 + r0 + rc, :] * w0 + gb[s, 7 + r0:7 + r0 + rc, :] * w1
                      + gb[s, 8 + r0:8 + r0 + rc, :] * w2)
                y = (cv * jax.nn.sigmoid(cv)) * ub[s * l + r0:s * l + r0 + rc, :]
                y_ref[s * l + r0:s * l + r0 + rc, cols] = y.astype(BF16)
            carry_out[s, :, cols] = gb[s, 6 + l:8 + l, :]


def _ffn_out(x1_ref, y_ref, w_down_ref, gfin_ref, o_ref, ns, l, rc):
    g = gfin_ref[...]
    for s in range(ns):
        for r0 in range(0, l, rc):
            rows = slice(s * l + r0, s * l + r0 + rc)
            x2 = x1_ref[rows, :] + _dot(y_ref[rows, :], w_down_ref[...])
            o_ref[s, r0:r0 + rc, :] = _rms(x2, g)


def _meta_kernel(sinks_ref, x_ref, bias_ref, gmix_ref, w_in_ref, dw_w_ref, dw_b_ref, ln_g_ref, ln_b_ref,
                 w_out_ref, gffn_ref, w_upg_ref,
                 glu_o, k_o, v_o, gate_o,
                 hb_ref, z_ref, glu_buf, mix_ref):
    l = N_META
    _in_proj(x_ref, gmix_ref, w_in_ref, hb_ref, z_ref, 1, l, l)
    glu_buf[0:HALO, :] = jnp.zeros((HALO, D_CONV), F32)
    _conv_branch(z_ref, 0, l, glu_buf, jnp.zeros((30, D_CONV), F32), dw_w_ref, dw_b_ref, ln_g_ref, ln_b_ref,
                 mix_ref, l)
    glu_o[...] = glu_buf[HALO:HALO + l, :]
    k = z_ref[:, _K0:_K0 + D_KV]
    v = z_ref[:, _V0:_V0 + D_KV]
    k_o[...] = k
    v_o[...] = v
    zb = jnp.zeros((BAND, D_KV), BF16)
    zp = jnp.zeros((NKEY - BAND - N_META, D_KV), BF16)
    k_parts = [jnp.concatenate([zb, p, zp], axis=0) for p in _lohi(k)]
    v_parts = [jnp.concatenate([zb, p, zp], axis=0) for p in _lohi(v)]
    o = _attn_block(z_ref[:, _Q0:_Q0 + D_ATTN], k_parts, v_parts, lambda g: bias_ref[0, g], sinks_ref)
    mix_ref[:, D_CONV:D_CONV + 256] = o[0].astype(BF16)
    mix_ref[:, D_CONV + 256:D_CONV + 512] = o[1].astype(BF16)
    x1 = x_ref[0] + _dot(mix_ref[...], w_out_ref[...])
    h2 = _rms(x1, gffn_ref[...]).astype(BF16)
    gate = _dot(h2, w_upg_ref[...])
    gate_o[...] = gate[N_META - 2:N_META, :]


def _run_meta(sinks, meta_tokens, bias_m, gmix, w_in, dw_w, dw_b, ln_g, ln_b, w_out, gffn, w_up):
    l = N_META
    full = lambda a: pl.BlockSpec(a.shape, lambda i: (0,) * a.ndim)
    x = meta_tokens[None]
    ins = [x, bias_m, gmix, w_in, dw_w, dw_b, ln_g, ln_b, w_out, gffn]
    in_specs = [pl.BlockSpec(memory_space=pltpu.SMEM)] + [full(a) for a in ins]
    in_specs.append(pl.BlockSpec((D_MODEL, D_FF), lambda i: (0, 0)))
    out_shape = [jax.ShapeDtypeStruct((l, D_CONV), F32), jax.ShapeDtypeStruct((l, D_KV), F32),
                 jax.ShapeDtypeStruct((l, D_KV), F32), jax.ShapeDtypeStruct((2, D_FF), F32)]
    return pl.pallas_call(
        _meta_kernel,
        grid=(1,),
        out_shape=out_shape,
        in_specs=in_specs,
        out_specs=[pl.BlockSpec(s.shape, lambda i: (0, 0)) for s in out_shape],
        scratch_shapes=[pltpu.VMEM((l, D_MODEL), BF16), pltpu.VMEM((l, D_IN), F32),
                        pltpu.VMEM((HALO + l, D_CONV), F32), pltpu.VMEM((l, D_MODEL), BF16)],
        compiler_params=pltpu.CompilerParams(vmem_limit_bytes=VMEM_LIMIT),
        name="meta_tokens",
    )(sinks, *ins, w_up)


def _prompt_kernel(sinks_ref, x_ref, glu_m_ref, k_m_ref, v_m_ref, gate_m_ref, bias_ref,
                   gmix_ref, w_in_ref, dw_w_ref, dw_b_ref, ln_g_ref, ln_b_ref, w_out_ref, gffn_ref,
                   w_up_ref, fdw_w_ref, fdw_b_ref, w_down_ref, gfin_ref,
                   y_o, kwin_o, vwin_o, conva_o, ffn_o,
                   hb_ref, z_ref, glu_buf, kv_buf, mix_ref, x1_ref, gbuf, ubuf, yh_ref):
    t = pl.program_id(1)
    l = TM
    rc = 128

    @pl.when(t == 0)
    def _():
        conva_o[0, 0:14, :] = jnp.zeros((14, D_CONV), F32)
        conva_o[0, 14:30, :] = glu_m_ref[...]
        kwin_o[0] = jnp.zeros((WINDOW, D_KV), F32)
        vwin_o[0] = jnp.zeros((WINDOW, D_KV), F32)
        ffn_o[0] = gate_m_ref[...]

    _in_proj(x_ref, gmix_ref, w_in_ref, hb_ref, z_ref, 1, l, rc)

    halo = _conv_branch(z_ref, 0, l, glu_buf, conva_o[0], dw_w_ref, dw_b_ref, ln_g_ref, ln_b_ref, mix_ref, 32)
    conva_o[0] = halo

    for idx, (win_o, c0) in enumerate(((kwin_o, _K0), (vwin_o, _V0))):
        for i, p in enumerate(_lohi(win_o[0])):
            kv_buf[4 * idx + i, 0:WINDOW, :] = p
        for r0 in range(0, l, rc):
            for i, p in enumerate(_lohi(z_ref[r0:r0 + rc, c0:c0 + D_KV])):
                kv_buf[4 * idx + i, WINDOW + r0:WINDOW + r0 + rc, :] = p
        win_o[0] = z_ref[l - WINDOW:l, c0:c0 + D_KV]
    zp = jnp.zeros((NKEY - BAND - N_META, D_KV), BF16)
    km = [jnp.concatenate([p, zp], axis=0) for p in _lohi(k_m_ref[...])]
    vm = [jnp.concatenate([p, zp], axis=0) for p in _lohi(v_m_ref[...])]

    for c in range(l // CHUNK):
        r0 = c * CHUNK
        variant = jnp.where(t == 0, c, 3) if c < 3 else 3
        k_parts = [jnp.concatenate([kv_buf[i, r0:r0 + BAND, :], km[i]], axis=0) for i in range(4)]
        v_parts = [jnp.concatenate([kv_buf[4 + i, r0:r0 + BAND, :], vm[i]], axis=0) for i in range(4)]
        o = _attn_block(z_ref[r0:r0 + CHUNK, _Q0:_Q0 + D_ATTN], k_parts, v_parts,
                        lambda g: bias_ref[variant, g], sinks_ref)
        mix_ref[r0:r0 + CHUNK, D_CONV:D_CONV + 256] = o[0].astype(BF16)
        mix_ref[r0:r0 + CHUNK, D_CONV + 256:D_CONV + 512] = o[1].astype(BF16)

    _out_proj(x_ref, mix_ref, w_out_ref, gffn_ref, x1_ref, hb_ref, 1, l, 256)
    _ffn_hidden(hb_ref, w_up_ref, fdw_w_ref, fdw_b_ref, gbuf, ubuf, yh_ref, ffn_o, ffn_o, 1, l, 64)
    _ffn_out(x1_ref, yh_ref, w_down_ref, gfin_ref, y_o, 1, l, 256)


def _const_spec(a):
    return pl.BlockSpec(a.shape, lambda *_: (0,) * a.ndim, pipeline_mode=pl.Buffered(1))


def _run_prompt(sinks, x, glu_m, k_m, v_m, gate_m, bias_p, weights):
    bsz, seq, _ = x.shape
    assert seq % TM == 0 and TM % CHUNK == 0 and TM >= WINDOW
    consts = [glu_m, k_m, v_m, gate_m, bias_p, *weights]
    in_specs = ([pl.BlockSpec(memory_space=pltpu.SMEM), pl.BlockSpec((1, TM, D_MODEL), lambda b, t: (b, t, 0))]
                + [_const_spec(a) for a in consts])
    out_shape = [jax.ShapeDtypeStruct((bsz, seq, D_MODEL), F32),
                 jax.ShapeDtypeStruct((bsz, WINDOW, D_KV), F32), jax.ShapeDtypeStruct((bsz, WINDOW, D_KV), F32),
                 jax.ShapeDtypeStruct((bsz, CONV_WIDTH - 1, D_CONV), F32),
                 jax.ShapeDtypeStruct((bsz, FFN_CONV_WIDTH - 1, D_FF), F32)]
    out_specs = [pl.BlockSpec((1, TM, D_MODEL), lambda b, t: (b, t, 0))] + [
        pl.BlockSpec((1,) + s.shape[1:], lambda b, t: (b, 0, 0)) for s in out_shape[1:]]
    scratch = [pltpu.VMEM((TM, D_MODEL), BF16), pltpu.VMEM((TM, D_IN), F32),
               pltpu.VMEM((HALO + TM, D_CONV), F32), pltpu.VMEM((8, WINDOW + TM, D_KV), BF16),
               pltpu.VMEM((TM, D_MODEL), BF16), pltpu.VMEM((TM, D_MODEL), F32),
               pltpu.VMEM((2, 1, 8 + TM, FFN_CW), F32), pltpu.VMEM((2, TM, FFN_CW), F32),
               pltpu.VMEM((TM, D_FF), BF16)]
    return pl.pallas_call(
        _prompt_kernel,
        grid=(bsz, seq // TM),
        out_shape=out_shape,
        in_specs=in_specs,
        out_specs=out_specs,
        scratch_shapes=scratch,
        compiler_params=pltpu.CompilerParams(dimension_semantics=("arbitrary", "arbitrary"),
                                             vmem_limit_bytes=VMEM_LIMIT),
        name="prompt_step",
    )(sinks, x, *consts)


def _sample_kernel(sinks_ref, x_ref, kmeta_ref, vmeta_ref, kwin_ref, vwin_ref, conva_ref, ffn_ref, bias_ref,
                   gmix_ref, w_in_ref, dw_w_ref, dw_b_ref, ln_g_ref, ln_b_ref, w_out_ref, gffn_ref,
                   w_up_ref, fdw_w_ref, fdw_b_ref, w_down_ref, gfin_ref,
                   y_o, kwin_o, vwin_o, conva_o, ffn_o,
                   hb_ref, z_ref, glu_buf, mix_ref, x1_ref, gbuf, ubuf, yh_ref):
    ns, l, _ = x_ref.shape
    _in_proj(x_ref, gmix_ref, w_in_ref, hb_ref, z_ref, ns, l, l)
    glu_buf[0:HALO - 30, :] = jnp.zeros((HALO - 30, D_CONV), F32)
    zgap = jnp.zeros((BAND - WINDOW - l, D_KV), BF16)
    zp = jnp.zeros((NKEY - BAND - N_META, D_KV), BF16)
    for s in range(ns):
        row0 = s * l
        conva_o[s] = _conv_branch(z_ref, row0, l, glu_buf, conva_ref[s], dw_w_ref, dw_b_ref, ln_g_ref, ln_b_ref,
                                  mix_ref, l)
        parts = []
        for win_ref, meta_ref, win_o, c0 in ((kwin_ref, kmeta_ref, kwin_o, _K0), (vwin_ref, vmeta_ref, vwin_o, _V0)):
            new = z_ref[row0:row0 + l, c0:c0 + D_KV]
            win = win_ref[s]
            win_o[s, 0:WINDOW - l, :] = win[l:, :]
            win_o[s, WINDOW - l:WINDOW, :] = new
            parts.append([jnp.concatenate([pw, pn, zgap, pm, zp], axis=0)
                          for pw, pn, pm in zip(_lohi(win), _lohi(new), _lohi(meta_ref[s]))])
        o = _attn_block(z_ref[row0:row0 + l, _Q0:_Q0 + D_ATTN], parts[0], parts[1],
                        lambda g: bias_ref[0, g], sinks_ref)
        mix_ref[row0:row0 + l, D_CONV:D_CONV + 256] = o[0].astype(BF16)
        mix_ref[row0:row0 + l, D_CONV + 256:D_CONV + 512] = o[1].astype(BF16)

    _out_proj(x_ref, mix_ref, w_out_ref, gffn_ref, x1_ref, hb_ref, ns, l, l)
    _ffn_hidden(hb_ref, w_up_ref, fdw_w_ref, fdw_b_ref, gbuf, ubuf, yh_ref, ffn_ref, ffn_o, ns, l, l)
    _ffn_out(x1_ref, yh_ref, w_down_ref, gfin_ref, y_o, ns, l, l)


def _run_sample(sinks, x, kmeta, vmeta, kwin, vwin, conva, ffn, bias_s, weights):
    nstream, l, _ = x.shape
    assert nstream % SB == 0 and l % 16 == 0 and l <= CHUNK and kwin.shape[1] == WINDOW
    rows = SB * l
    per_stream = [x, kmeta, vmeta, kwin, vwin, conva, ffn]
    consts = [bias_s, *weights]
    in_specs = ([pl.BlockSpec(memory_space=pltpu.SMEM)]
                + [pl.BlockSpec((SB,) + a.shape[1:], lambda i: (i, 0, 0)) for a in per_stream]
                + [_const_spec(a) for a in consts])
    out_shape = [jax.ShapeDtypeStruct(a.shape, F32) for a in (x, kwin, vwin, conva, ffn)]
    out_specs = [pl.BlockSpec((SB,) + s.shape[1:], lambda i: (i, 0, 0)) for s in out_shape]
    scratch = [pltpu.VMEM((rows, D_MODEL), BF16), pltpu.VMEM((rows, D_IN), F32),
               pltpu.VMEM((HALO + l, D_CONV), F32),
               pltpu.VMEM((rows, D_MODEL), BF16), pltpu.VMEM((rows, D_MODEL), F32),
               pltpu.VMEM((2, SB, 8 + l, FFN_CW), F32), pltpu.VMEM((2, rows, FFN_CW), F32),
               pltpu.VMEM((rows, D_FF), BF16)]
    return pl.pallas_call(
        _sample_kernel,
        grid=(nstream // SB,),
        out_shape=out_shape,
        in_specs=in_specs,
        out_specs=out_specs,
        scratch_shapes=scratch,
        compiler_params=pltpu.CompilerParams(dimension_semantics=("arbitrary",), vmem_limit_bytes=VMEM_LIMIT),
        name="sample_step",
    )(sinks, *per_stream, *consts)


def kernel(x_prompt, x_sample, cache_k_meta, cache_v_meta, cache_k_win, cache_v_win, state_conv_a, state_conv_ffn,
           meta_tokens, rel_bias_table, norm_mix, w_in, conv_dw_w, conv_dw_b, conv_ln_g, conv_ln_b, attn_sinks,
           w_out, norm_ffn, w_up, ffn_dw_w, ffn_dw_b, w_down, norm_final):
    depth = w_in.shape[0]
    assert depth == 1, "single-layer step"
    bsz = x_prompt.shape[0]
    nstream, dec_seq, _ = x_sample.shape
    win = cache_k_win.shape[2]
    assert win == WINDOW

    bp, bs, bm = _bucket_tables(PAST_LEN, dec_seq)
    bias_p, bias_s, bias_m = _build_bias(rel_bias_table, jnp.asarray(bp), jnp.asarray(bs), jnp.asarray(bm))

    row = lambda a: a.reshape(1, -1)
    w_in_b = w_in[0].astype(BF16)
    w_out_b = w_out[0].astype(BF16)
    w_up_b = w_up[0].astype(BF16)
    w_down_b = w_down[0].astype(BF16)
    sinks = attn_sinks[0]
    gmix, gffn, gfin = row(norm_mix[0]), row(norm_ffn[0]), row(norm_final)
    dw_w, dw_b = conv_dw_w[0], row(conv_dw_b[0])
    ln_g, ln_b = row(conv_ln_g[0]), row(conv_ln_b[0])
    fdw_w, fdw_b = ffn_dw_w[0], row(ffn_dw_b[0])

    glu_m, k_m, v_m, gate_m = _run_meta(sinks, meta_tokens, bias_m, gmix, w_in_b, dw_w, dw_b, ln_g, ln_b,
                                        w_out_b, gffn, w_up_b)

    weights = [gmix, w_in_b, dw_w, dw_b, ln_g, ln_b, w_out_b, gffn, w_up_b, fdw_w, fdw_b, w_down_b, gfin]
    y_p, kwin_p, vwin_p, conva_p, ffn_p = _run_prompt(sinks, x_prompt, glu_m, k_m, v_m, gate_m, bias_p, weights)

    kv = lambda a: a.reshape(a.shape[0], a.shape[1], D_KV)
    y_s, kwin_s, vwin_s, conva_s, ffn_s = _run_sample(
        sinks, x_sample, kv(cache_k_meta[0]), kv(cache_v_meta[0]), kv(cache_k_win[0]), kv(cache_v_win[0]),
        state_conv_a[0], state_conv_ffn[0], bias_s, weights)

    heads = lambda a: a.reshape(1, a.shape[0], a.shape[1], N_KV_HEADS, HEAD_DIM)
    meta_b = lambda a: jnp.broadcast_to(a.reshape(1, 1, N_META, N_KV_HEADS, HEAD_DIM),
                                        (1, bsz, N_META, N_KV_HEADS, HEAD_DIM))
    return (y_p, y_s, meta_b(k_m), meta_b(v_m), heads(kwin_p), heads(vwin_p), conva_p[None], ffn_p[None],
            heads(kwin_s), heads(vwin_s), conva_s[None], ffn_s[None])
```

```python
import functools
import math

import numpy as np
import jax
import jax.numpy as jnp
from jax import lax
from jax.experimental import pallas as pl
from jax.experimental.pallas import tpu as pltpu

F32 = jnp.float32
BF16 = jnp.bfloat16

D_MODEL = 1024
CHUNK = 64
N_META = 16
D_CONV = 512
CONV_WIDTH = 31
N_HEADS = 8
N_KV_HEADS = 2
HEAD_DIM = 64
D_ATTN = N_HEADS * HEAD_DIM
D_KV = N_KV_HEADS * HEAD_DIM
WINDOW = 128
PAST_LEN = 4096
N_BUCKETS = 32
MAX_DISTANCE = 256
D_FF = 2816
FFN_CONV_WIDTH = 3
D_IN = 2 * D_CONV + D_ATTN + 2 * D_KV
EPS = 1e-6
NEG_INF = -1e30
SCALE = HEAD_DIM ** -0.5

_Q0 = 2 * D_CONV
_K0 = _Q0 + D_ATTN
_V0 = _K0 + D_KV

NKEY = 256
BAND = WINDOW + CHUNK
HALO = 32
FFN_CW = 256
TM = 512
SB = 16
VMEM_LIMIT = 60 * 1024 * 1024


def _np_bucket(rel):
    nb = N_BUCKETS // 2
    max_exact = nb // 2
    ret = np.where(rel > 0, nb, 0)
    n = np.abs(rel)
    nf = np.maximum(n, 1).astype(np.float32)
    large = max_exact + (np.log(nf / np.float32(max_exact)) / np.float32(math.log(MAX_DISTANCE / max_exact))
                         * np.float32(nb - max_exact)).astype(np.int32)
    large = np.minimum(large, nb - 1)
    return (ret + np.where(n < max_exact, n, large)).astype(np.int32)


def _bucket_tables(past_len, dec_seq):
    i = np.arange(CHUNK)[:, None]
    j = np.arange(BAND)[None, :]
    m = np.arange(N_META)[None, :]
    band = _np_bucket(j - WINDOW - i)
    pad = np.full((CHUNK, NKEY - BAND - N_META), -1, np.int32)

    def meta_for(c):
        return _np_bucket(m - (N_META + CHUNK * c + i))

    prompt = []
    for c in range(4):
        b = np.where(c - 2 + j // CHUNK >= 0, band, -1)
        prompt.append(np.concatenate([b, meta_for(c), pad], axis=1))
    assert np.array_equal(meta_for(3), meta_for(4096))
    prompt = np.stack(prompt).astype(np.int32)

    i_s = np.arange(dec_seq)[:, None]
    band_s = np.where(j < WINDOW + dec_seq, _np_bucket(j - WINDOW - i_s), -1)
    meta_s = _np_bucket(m - (N_META + past_len + i_s))
    sample = np.concatenate([band_s, meta_s, pad[:dec_seq]], axis=1)[None].astype(np.int32)

    i_m = np.arange(N_META)[:, None]
    meta_self = np.concatenate([np.full((N_META, BAND), -1, np.int32), _np_bucket(m - i_m), pad[:N_META]],
                               axis=1)[None].astype(np.int32)
    return prompt, sample, meta_self


def _bias_kernel(table_ref, bp_ref, bs_ref, bm_ref, op_ref, os_ref, om_ref):
    for b_ref, o_ref in ((bp_ref, op_ref), (bs_ref, os_ref), (bm_ref, om_ref)):
        nv, r, _ = b_ref.shape
        for v in range(nv):
            bk = b_ref[v]
            for g in range(2):
                for pp in range(2):
                    for hf in range(2):
                        h = 4 * g + 2 * pp + hf
                        acc = jnp.full((r, NKEY), NEG_INF, F32)
                        for b in range(N_BUCKETS):
                            acc = jnp.where(bk == b, table_ref[b, h], acc)
                        o_ref[v, g, r * pp:r * (pp + 1), NKEY * hf:NKEY * (hf + 1)] = acc


def _build_bias(table, bp, bs, bm):
    shapes = [jax.ShapeDtypeStruct((b.shape[0], 2, 2 * b.shape[1], 2 * NKEY), F32) for b in (bp, bs, bm)]
    vm = pl.BlockSpec(memory_space=pltpu.VMEM)
    return pl.pallas_call(
        _bias_kernel,
        out_shape=shapes,
        in_specs=[pl.BlockSpec(memory_space=pltpu.SMEM), vm, vm, vm],
        out_specs=[vm, vm, vm],
        name="rel_bias",
    )(table, bp, bs, bm)


def _dot(a, b):
    return jnp.dot(a, b, preferred_element_type=F32)


def _dot_nt(a, b):
    return lax.dot_general(a, b, (((1,), (1,)), ((), ())), preferred_element_type=F32)


def _rms(x, g):
    ms = jnp.mean(x * x, axis=-1, keepdims=True)
    return (x * lax.rsqrt(ms + EPS)) * g


def _lohi(x):
    lo = lax.broadcasted_iota(jnp.int32, x.shape, 1) < HEAD_DIM
    xr = pltpu.roll(x, HEAD_DIM, axis=1)
    zero = jnp.zeros_like(x)
    return (jnp.where(lo, x, zero).astype(BF16), jnp.where(lo, zero, xr).astype(BF16),
            jnp.where(lo, xr, zero).astype(BF16), jnp.where(lo, zero, x).astype(BF16))


def _attn_block(q, k_parts, v_parts, bias_g, sinks_ref):
    r = q.shape[0]
    outs = []
    for g in range(N_KV_HEADS):
        qg = jnp.concatenate([q[:, 256 * g:256 * g + 128], q[:, 256 * g + 128:256 * g + 256]], axis=0).astype(BF16)
        kk = jnp.concatenate([k_parts[2 * g], k_parts[2 * g + 1]], axis=0)
        vv = jnp.concatenate([v_parts[2 * g], v_parts[2 * g + 1]], axis=0)
        s = _dot_nt(qg, kk) * SCALE + bias_g(g)
        rows = []
        for pp in range(2):
            segs = []
            for hf in range(2):
                sink = sinks_ref[4 * g + 2 * pp + hf]
                seg = s[r * pp:r * (pp + 1), NKEY * hf:NKEY * (hf + 1)]
                mx = jnp.maximum(jnp.max(seg, axis=-1, keepdims=True), sink)
                e = jnp.exp(seg - mx)
                den = jnp.sum(e, axis=-1, keepdims=True) + jnp.exp(sink - mx)
                segs.append((e * (1.0 / den)).astype(BF16))
            rows.append(jnp.concatenate(segs, axis=1))
        p = jnp.concatenate(rows, axis=0)
        o = _dot(p, vv)
        outs.append(jnp.concatenate([o[:r], o[r:]], axis=1))
    return outs


def _in_proj(x_ref, gmix_ref, w_in_ref, hb_ref, z_ref, ns, l, rc):
    g = gmix_ref[...]
    for s in range(ns):
        for r0 in range(0, l, rc):
            hb_ref[s * l + r0:s * l + r0 + rc, :] = _rms(x_ref[s, r0:r0 + rc, :], g).astype(BF16)
    z_ref[...] = _dot(hb_ref[...], w_in_ref[...])


def _delay(x):
    return pltpu.roll(x, 1, axis=0)


def _conv31_block(glu_buf, u0, rc, cw_ref, l0, n):
    nb = (rc + 8) // 8
    acc = None
    for r in range(7, -1, -1):
        s = None
        for a in range(4):
            d = 8 * a + r
            if d >= CONV_WIDTH:
                continue
            blk = glu_buf[u0 - 8 - 8 * a:u0 + rc - 8 * a, l0:l0 + n].reshape(nb, 8, n)
            term = blk * cw_ref[d, :, l0:l0 + n][None]
            s = term if s is None else s + term
        s = s.reshape(rc + 8, n)
        acc = s if acc is None else s + _delay(acc)
    return acc[8:]


def _conv_branch(z_ref, row0, l, glu_buf, cw_ref, dw_b_ref, ln_g_ref, ln_b_ref, mix_ref, rc):
    for r0 in range(0, l, rc):
        a = z_ref[row0 + r0:row0 + r0 + rc, 0:D_CONV]
        b = z_ref[row0 + r0:row0 + r0 + rc, D_CONV:2 * D_CONV]
        glu_buf[HALO + r0:HALO + r0 + rc, :] = a * jax.nn.sigmoid(b)
    for r0 in range(0, l, rc):
        for l0 in range(0, D_CONV, 128):
            z_ref[row0 + r0:row0 + r0 + rc, l0:l0 + 128] = _conv31_block(glu_buf, HALO + r0, rc, cw_ref, l0, 128)
    bias = dw_b_ref[...]
    ln_g = ln_g_ref[...]
    ln_b = ln_b_ref[...]
    rn = min(l, 64)
    for r0 in range(0, l, rn):
        acc = z_ref[row0 + r0:row0 + r0 + rn, 0:D_CONV] + bias
        mu = jnp.mean(acc, axis=-1, keepdims=True)
        d = acc - mu
        var = jnp.mean(d * d, axis=-1, keepdims=True)
        yn = (d * lax.rsqrt(var + EPS)) * ln_g + ln_b
        mix_ref[row0 + r0:row0 + r0 + rn, 0:D_CONV] = (yn * jax.nn.sigmoid(yn)).astype(BF16)


def _out_proj(x_ref, mix_ref, w_out_ref, gffn_ref, x1_ref, hb_ref, ns, l, rc):
    g = gffn_ref[...]
    for s in range(ns):
        for r0 in range(0, l, rc):
            rows = slice(s * l + r0, s * l + r0 + rc)
            x1 = x_ref[s, r0:r0 + rc, :] + _dot(mix_ref[rows, :], w_out_ref[...])
            x1_ref[rows, :] = x1
            hb_ref[rows, :] = _rms(x1, g).astype(BF16)


def _ffn_hidden(hb_ref, w_up_ref, fw_ref, fb_ref, gbuf, ubuf, y_ref, put_halo, save_carry, ns, l, rc):
    nch = D_FF // FFN_CW
    nb = (rc + 8) // 8

    def matmuls(ci):
        c = ci * FFN_CW
        gb = gbuf.at[ci % 2]
        ug = _dot(hb_ref[...], w_up_ref[:, c:c + FFN_CW])
        for s in range(ns):
            put_halo(gb, s, slice(c, c + FFN_CW))
            gb[s, 8:8 + l, :] = ug[s * l:(s + 1) * l]
        ubuf[ci % 2] = _dot(hb_ref[...], w_up_ref[:, D_FF + c:D_FF + c + FFN_CW])

    def elementwise(ci):
        c = ci * FFN_CW
        gb = gbuf.at[ci % 2]
        ub = ubuf.at[ci % 2]
        cols = slice(c, c + FFN_CW)
        w0 = fw_ref[0, :, cols][None]
        w1 = fw_ref[1, :, cols][None]
        w2 = fw_ref[2, :, cols][None]
        bb = fb_ref[:, cols][None]
        for s in range(ns):
            for r0 in range(0, l, rc):
                g = gb[s, r0:r0 + rc + 8, :].reshape(nb, 8, FFN_CW)
                t = (g * w0).reshape(rc + 8, FFN_CW)
                t = (g * w1).reshape(rc + 8, FFN_CW) + _delay(t)
                t = (g * w2 + bb).reshape(rc + 8, FFN_CW) + _delay(t)
                cv = t[8:]
                y = (cv * jax.nn.sigmoid(cv)) * ub[s * l + r0:s * l + r0 + rc, :]
                y_ref[s * l + r0:s * l + r0 + rc, cols] = y.astype(BF16)
            save_carry(gb, s, cols)

    matmuls(0)
    for ci in range(nch):
        if ci + 1 < nch:
            matmuls(ci + 1)
        elementwise(ci)


def _ffn_out(x1_ref, y_ref, w_down_ref, gfin_ref, o_ref, ns, l, rc):
    g = gfin_ref[...]
    for s in range(ns):
        for r0 in range(0, l, rc):
            rows = slice(s * l + r0, s * l + r0 + rc)
            x2 = x1_ref[rows, :] + _dot(y_ref[rows, :], w_down_ref[...])
            o_ref[s, r0:r0 + rc, :] = _rms(x2, g)


def _meta_kernel(sinks_ref, x_ref, bias_ref, gmix_ref, w_in_ref, cw_ref, dw_b_ref, ln_g_ref, ln_b_ref,
                 w_out_ref, gffn_ref, w_upg_ref,
                 glu_o, k_o, v_o, gate_o,
                 hb_ref, z_ref, glu_buf, mix_ref):
    l = N_META
    _in_proj(x_ref, gmix_ref, w_in_ref, hb_ref, z_ref, 1, l, l)
    glu_buf[0:HALO, :] = jnp.zeros((HALO, D_CONV), F32)
    _conv_branch(z_ref, 0, l, glu_buf, cw_ref, dw_b_ref, ln_g_ref, ln_b_ref, mix_ref, l)
    glu_o[...] = glu_buf[HALO:HALO + l, :]
    k = z_ref[:, _K0:_K0 + D_KV]
    v = z_ref[:, _V0:_V0 + D_KV]
    k_o[...] = k
    v_o[...] = v
    zb = jnp.zeros((BAND, D_KV), BF16)
    zp = jnp.zeros((NKEY - BAND - N_META, D_KV), BF16)
    k_parts = [jnp.concatenate([zb, p, zp], axis=0) for p in _lohi(k)]
    v_parts = [jnp.concatenate([zb, p, zp], axis=0) for p in _lohi(v)]
    o = _attn_block(z_ref[:, _Q0:_Q0 + D_ATTN], k_parts, v_parts, lambda g: bias_ref[0, g], sinks_ref)
    mix_ref[:, D_CONV:D_CONV + 256] = o[0].astype(BF16)
    mix_ref[:, D_CONV + 256:D_CONV + 512] = o[1].astype(BF16)
    x1 = x_ref[0] + _dot(mix_ref[...], w_out_ref[...])
    h2 = _rms(x1, gffn_ref[...]).astype(BF16)
    gate = _dot(h2, w_upg_ref[...])
    gate_o[...] = gate[N_META - 8:N_META, :]


def _run_meta(sinks, meta_tokens, bias_m, gmix, w_in, cw, dw_b, ln_g, ln_b, w_out, gffn, w_up):
    l = N_META
    full = lambda a: pl.BlockSpec(a.shape, lambda i: (0,) * a.ndim)
    x = meta_tokens[None]
    ins = [x, bias_m, gmix, w_in, cw, dw_b, ln_g, ln_b, w_out, gffn]
    in_specs = [pl.BlockSpec(memory_space=pltpu.SMEM)] + [full(a) for a in ins]
    in_specs.append(pl.BlockSpec((D_MODEL, D_FF), lambda i: (0, 0)))
    out_shape = [jax.ShapeDtypeStruct((l, D_CONV), F32), jax.ShapeDtypeStruct((l, D_KV), F32),
                 jax.ShapeDtypeStruct((l, D_KV), F32), jax.ShapeDtypeStruct((8, D_FF), F32)]
    return pl.pallas_call(
        _meta_kernel,
        grid=(1,),
        out_shape=out_shape,
        in_specs=in_specs,
        out_specs=[pl.BlockSpec(s.shape, lambda i: (0, 0)) for s in out_shape],
        scratch_shapes=[pltpu.VMEM((l, D_MODEL), BF16), pltpu.VMEM((l, D_IN), F32),
                        pltpu.VMEM((HALO + l, D_CONV), F32), pltpu.VMEM((l, D_MODEL), BF16)],
        compiler_params=pltpu.CompilerParams(vmem_limit_bytes=VMEM_LIMIT),
        name="meta_tokens",
    )(sinks, *ins, w_up)


def _prompt_kernel(sinks_ref, x_ref, glu_m_ref, k_m_ref, v_m_ref, gate_m_ref, bias_ref,
                   gmix_ref, w_in_ref, cw_ref, dw_b_ref, ln_g_ref, ln_b_ref, w_out_ref, gffn_ref,
                   w_up_ref, fw_ref, fb_ref, w_down_ref, gfin_ref,
                   y_o, kwin_o, vwin_o, conva_o, ffn_o,
                   hb_ref, z_ref, glu_buf, kv_buf, mix_ref, x1_ref, gbuf, ubuf, yh_ref, gcarry):
    t = pl.program_id(1)
    l = TM
    rc = 128

    @pl.when(t == 0)
    def _():
        glu_buf[0:HALO - N_META, :] = jnp.zeros((HALO - N_META, D_CONV), F32)
        glu_buf[HALO - N_META:HALO, :] = glu_m_ref[...]
        kwin_o[0] = jnp.zeros((WINDOW, D_KV), F32)
        vwin_o[0] = jnp.zeros((WINDOW, D_KV), F32)
        gcarry[...] = gate_m_ref[...]

    _in_proj(x_ref, gmix_ref, w_in_ref, hb_ref, z_ref, 1, l, rc)

    _conv_branch(z_ref, 0, l, glu_buf, cw_ref, dw_b_ref, ln_g_ref, ln_b_ref, mix_ref, rc)
    conva_o[0] = glu_buf[HALO + l - (CONV_WIDTH - 1):HALO + l, :]
    glu_buf[0:HALO, :] = glu_buf[l:l + HALO, :]

    for idx, (win_o, c0) in enumerate(((kwin_o, _K0), (vwin_o, _V0))):
        for i, p in enumerate(_lohi(win_o[0])):
            kv_buf[4 * idx + i, 0:WINDOW, :] = p
        for r0 in range(0, l, rc):
            for i, p in enumerate(_lohi(z_ref[r0:r0 + rc, c0:c0 + D_KV])):
                kv_buf[4 * idx + i, WINDOW + r0:WINDOW + r0 + rc, :] = p
        win_o[0] = z_ref[l - WINDOW:l, c0:c0 + D_KV]
    zp = jnp.zeros((NKEY - BAND - N_META, D_KV), BF16)
    km = [jnp.concatenate([p, zp], axis=0) for p in _lohi(k_m_ref[...])]
    vm = [jnp.concatenate([p, zp], axis=0) for p in _lohi(v_m_ref[...])]

    for c in range(l // CHUNK):
        r0 = c * CHUNK
        variant = jnp.where(t == 0, c, 3) if c < 3 else 3
        k_parts = [jnp.concatenate([kv_buf[i, r0:r0 + BAND, :], km[i]], axis=0) for i in range(4)]
        v_parts = [jnp.concatenate([kv_buf[4 + i, r0:r0 + BAND, :], vm[i]], axis=0) for i in range(4)]
        o = _attn_block(z_ref[r0:r0 + CHUNK, _Q0:_Q0 + D_ATTN], k_parts, v_parts,
                        lambda g: bias_ref[variant, g], sinks_ref)
        mix_ref[r0:r0 + CHUNK, D_CONV:D_CONV + 256] = o[0].astype(BF16)
        mix_ref[r0:r0 + CHUNK, D_CONV + 256:D_CONV + 512] = o[1].astype(BF16)

    _out_proj(x_ref, mix_ref, w_out_ref, gffn_ref, x1_ref, hb_ref, 1, l, 256)

    def put_halo(gb, s, cols):
        gb[s, 0:8, :] = gcarry[:, cols]

    def save_carry(gb, s, cols):
        gcarry[:, cols] = gb[s, l:l + 8, :]

    _ffn_hidden(hb_ref, w_up_ref, fw_ref, fb_ref, gbuf, ubuf, yh_ref, put_halo, save_carry, 1, l, rc)
    ffn_o[0] = gcarry[8 - (FFN_CONV_WIDTH - 1):8, :]
    _ffn_out(x1_ref, yh_ref, w_down_ref, gfin_ref, y_o, 1, l, 256)


def _const_spec(a):
    return pl.BlockSpec(a.shape, lambda *_: (0,) * a.ndim, pipeline_mode=pl.Buffered(1))


def _run_prompt(sinks, x, glu_m, k_m, v_m, gate_m, bias_p, weights):
    bsz, seq, _ = x.shape
    assert seq % TM == 0 and TM % CHUNK == 0 and TM >= WINDOW
    consts = [glu_m, k_m, v_m, gate_m, bias_p, *weights]
    in_specs = ([pl.BlockSpec(memory_space=pltpu.SMEM), pl.BlockSpec((1, TM, D_MODEL), lambda b, t: (b, t, 0))]
                + [_const_spec(a) for a in consts])
    out_shape = [jax.ShapeDtypeStruct((bsz, seq, D_MODEL), F32),
                 jax.ShapeDtypeStruct((bsz, WINDOW, D_KV), F32), jax.ShapeDtypeStruct((bsz, WINDOW, D_KV), F32),
                 jax.ShapeDtypeStruct((bsz, CONV_WIDTH - 1, D_CONV), F32),
                 jax.ShapeDtypeStruct((bsz, FFN_CONV_WIDTH - 1, D_FF), F32)]
    out_specs = [pl.BlockSpec((1, TM, D_MODEL), lambda b, t: (b, t, 0))] + [
        pl.BlockSpec((1,) + s.shape[1:], lambda b, t: (b, 0, 0)) for s in out_shape[1:]]
    scratch = [pltpu.VMEM((TM, D_MODEL), BF16), pltpu.VMEM((TM, D_IN), F32),
               pltpu.VMEM((HALO + TM, D_CONV), F32), pltpu.VMEM((8, WINDOW + TM, D_KV), BF16),
               pltpu.VMEM((TM, D_MODEL), BF16), pltpu.VMEM((TM, D_MODEL), F32),
               pltpu.VMEM((2, 1, 8 + TM, FFN_CW), F32), pltpu.VMEM((2, TM, FFN_CW), F32),
               pltpu.VMEM((TM, D_FF), BF16), pltpu.VMEM((8, D_FF), F32)]
    return pl.pallas_call(
        _prompt_kernel,
        grid=(bsz, seq // TM),
        out_shape=out_shape,
        in_specs=in_specs,
        out_specs=out_specs,
        scratch_shapes=scratch,
        compiler_params=pltpu.CompilerParams(dimension_semantics=("arbitrary", "arbitrary"),
                                             vmem_limit_bytes=VMEM_LIMIT),
        name="prompt_step",
    )(sinks, x, *consts)


def _sample_kernel(sinks_ref, x_ref, kmeta_ref, vmeta_ref, kwin_ref, vwin_ref, conva_ref, ffn_ref, bias_ref,
                   gmix_ref, w_in_ref, cw_ref, dw_b_ref, ln_g_ref, ln_b_ref, w_out_ref, gffn_ref,
                   w_up_ref, fw_ref, fb_ref, w_down_ref, gfin_ref,
                   y_o, kwin_o, vwin_o, conva_o, ffn_o,
                   hb_ref, z_ref, glu_buf, mix_ref, x1_ref, gbuf, ubuf, yh_ref):
    ns, l, _ = x_ref.shape
    _in_proj(x_ref, gmix_ref, w_in_ref, hb_ref, z_ref, ns, l, l)
    glu_buf[0:HALO - 30, :] = jnp.zeros((HALO - 30, D_CONV), F32)
    zgap = jnp.zeros((BAND - WINDOW - l, D_KV), BF16)
    zp = jnp.zeros((NKEY - BAND - N_META, D_KV), BF16)
    for s in range(ns):
        row0 = s * l
        glu_buf[HALO - (CONV_WIDTH - 1):HALO, :] = conva_ref[s]
        _conv_branch(z_ref, row0, l, glu_buf, cw_ref, dw_b_ref, ln_g_ref, ln_b_ref, mix_ref, l)
        conva_o[s] = glu_buf[HALO + l - (CONV_WIDTH - 1):HALO + l, :]
        parts = []
        for win_ref, meta_ref, win_o, c0 in ((kwin_ref, kmeta_ref, kwin_o, _K0), (vwin_ref, vmeta_ref, vwin_o, _V0)):
            new = z_ref[row0:row0 + l, c0:c0 + D_KV]
            win = win_ref[s]
            win_o[s, 0:WINDOW - l, :] = win[l:, :]
            win_o[s, WINDOW - l:WINDOW, :] = new
            parts.append([jnp.concatenate([pw, pn, zgap, pm, zp], axis=0)
                          for pw, pn, pm in zip(_lohi(win), _lohi(new), _lohi(meta_ref[s]))])
        o = _attn_block(z_ref[row0:row0 + l, _Q0:_Q0 + D_ATTN], parts[0], parts[1],
                        lambda g: bias_ref[0, g], sinks_ref)
        mix_ref[row0:row0 + l, D_CONV:D_CONV + 256] = o[0].astype(BF16)
        mix_ref[row0:row0 + l, D_CONV + 256:D_CONV + 512] = o[1].astype(BF16)

    _out_proj(x_ref, mix_ref, w_out_ref, gffn_ref, x1_ref, hb_ref, ns, l, l)

    nprev = FFN_CONV_WIDTH - 1
    for i in range(2):
        for s in range(ns):
            gbuf[i, s, 0:8, :] = jnp.zeros((8, FFN_CW), F32)

    def put_halo(gb, s, cols):
        gb[s, 8 - nprev:8, :] = ffn_ref[s, :, cols]

    def save_carry(gb, s, cols):
        ffn_o[s, :, cols] = gb[s, 8 + l - nprev:8 + l, :]

    _ffn_hidden(hb_ref, w_up_ref, fw_ref, fb_ref, gbuf, ubuf, yh_ref, put_halo, save_carry, ns, l, l)
    _ffn_out(x1_ref, yh_ref, w_down_ref, gfin_ref, y_o, ns, l, l)


def _run_sample(sinks, x, kmeta, vmeta, kwin, vwin, conva, ffn, bias_s, weights):
    nstream, l, _ = x.shape
    assert nstream % SB == 0 and l % 16 == 0 and l <= CHUNK and kwin.shape[1] == WINDOW
    rows = SB * l
    per_stream = [x, kmeta, vmeta, kwin, vwin, conva, ffn]
    consts = [bias_s, *weights]
    in_specs = ([pl.BlockSpec(memory_space=pltpu.SMEM)]
                + [pl.BlockSpec((SB,) + a.shape[1:], lambda i: (i, 0, 0)) for a in per_stream]
                + [_const_spec(a) for a in consts])
    out_shape = [jax.ShapeDtypeStruct(a.shape, F32) for a in (x, kwin, vwin, conva, ffn)]
    out_specs = [pl.BlockSpec((SB,) + s.shape[1:], lambda i: (i, 0, 0)) for s in out_shape]
    scratch = [pltpu.VMEM((rows, D_MODEL), BF16), pltpu.VMEM((rows, D_IN), F32),
               pltpu.VMEM((HALO + l, D_CONV), F32),
               pltpu.VMEM((rows, D_MODEL), BF16), pltpu.VMEM((rows, D_MODEL), F32),
               pltpu.VMEM((2, SB, 8 + l, FFN_CW), F32), pltpu.VMEM((2, rows, FFN_CW), F32),
               pltpu.VMEM((rows, D_FF), BF16)]
    return pl.pallas_call(
        _sample_kernel,
        grid=(nstream // SB,),
        out_shape=out_shape,
        in_specs=in_specs,
        out_specs=out_specs,
        scratch_shapes=scratch,
        compiler_params=pltpu.CompilerParams(dimension_semantics=("arbitrary",), vmem_limit_bytes=VMEM_LIMIT),
        name="sample_step",
    )(sinks, *per_stream, *consts)


def kernel(x_prompt, x_sample, cache_k_meta, cache_v_meta, cache_k_win, cache_v_win, state_conv_a, state_conv_ffn,
           meta_tokens, rel_bias_table, norm_mix, w_in, conv_dw_w, conv_dw_b, conv_ln_g, conv_ln_b, attn_sinks,
           w_out, norm_ffn, w_up, ffn_dw_w, ffn_dw_b, w_down, norm_final):
    depth = w_in.shape[0]
    assert depth == 1, "single-layer step"
    bsz = x_prompt.shape[0]
    nstream, dec_seq, _ = x_sample.shape
    win = cache_k_win.shape[2]
    assert win == WINDOW

    bp, bs, bm = _bucket_tables(PAST_LEN, dec_seq)
    bias_p, bias_s, bias_m = _build_bias(rel_bias_table, jnp.asarray(bp), jnp.asarray(bs), jnp.asarray(bm))

    row = lambda a: a.reshape(1, -1)
    w_in_b = w_in[0].astype(BF16)
    w_out_b = w_out[0].astype(BF16)
    w_up_b = w_up[0].astype(BF16)
    w_down_b = w_down[0].astype(BF16)
    sinks = attn_sinks[0]
    gmix, gffn, gfin = row(norm_mix[0]), row(norm_ffn[0]), row(norm_final)
    cw = jnp.broadcast_to(conv_dw_w[0][::-1][:, None, :], (CONV_WIDTH, 8, D_CONV))
    dw_b = row(conv_dw_b[0])
    ln_g, ln_b = row(conv_ln_g[0]), row(conv_ln_b[0])
    fw = jnp.broadcast_to(ffn_dw_w[0][:, None, :], (FFN_CONV_WIDTH, 8, D_FF))
    fb = jnp.broadcast_to(ffn_dw_b[0][None, :], (8, D_FF))

    glu_m, k_m, v_m, gate_m = _run_meta(sinks, meta_tokens, bias_m, gmix, w_in_b, cw, dw_b, ln_g, ln_b,
                                        w_out_b, gffn, w_up_b)

    weights = [gmix, w_in_b, cw, dw_b, ln_g, ln_b, w_out_b, gffn, w_up_b, fw, fb, w_down_b, gfin]
    y_p, kwin_p, vwin_p, conva_p, ffn_p = _run_prompt(sinks, x_prompt, glu_m, k_m, v_m, gate_m, bias_p, weights)

    kv = lambda a: a.reshape(a.shape[0], a.shape[1], D_KV)
    y_s, kwin_s, vwin_s, conva_s, ffn_s = _run_sample(
        sinks, x_sample, kv(cache_k_meta[0]), kv(cache_v_meta[0]), kv(cache_k_win[0]), kv(cache_v_win[0]),
        state_conv_a[0], state_conv_ffn[0], bias_s, weights)

    heads = lambda a: a.reshape(1, a.shape[0], a.shape[1], N_KV_HEADS, HEAD_DIM)
    meta_b = lambda a: jnp.broadcast_to(a.reshape(1, 1, N_META, N_KV_HEADS, HEAD_DIM),
                                        (1, bsz, N_META, N_KV_HEADS, HEAD_DIM))
    return (y_p, y_s, meta_b(k_m), meta_b(v_m), heads(kwin_p), heads(vwin_p), conva_p[None], ffn_p[None],
            heads(kwin_s), heads(vwin_s), conva_s[None], ffn_s[None])
```

```python
import functools
import math

import numpy as np
import jax
import jax.numpy as jnp
from jax import lax
from jax.experimental import pallas as pl
from jax.experimental.pallas import tpu as pltpu

F32 = jnp.float32
BF16 = jnp.bfloat16

D_MODEL = 1024
CHUNK = 64
N_META = 16
D_CONV = 512
CONV_WIDTH = 31
N_HEADS = 8
N_KV_HEADS = 2
HEAD_DIM = 64
D_ATTN = N_HEADS * HEAD_DIM
D_KV = N_KV_HEADS * HEAD_DIM
WINDOW = 128
PAST_LEN = 4096
N_BUCKETS = 32
MAX_DISTANCE = 256
D_FF = 2816
FFN_CONV_WIDTH = 3
D_IN = 2 * D_CONV + D_ATTN + 2 * D_KV
EPS = 1e-6
NEG_INF = -1e30
SCALE = HEAD_DIM ** -0.5

_Q0 = 2 * D_CONV
_K0 = _Q0 + D_ATTN
_V0 = _K0 + D_KV

NKEY = 256
BAND = WINDOW + CHUNK
HALO = 32
FFN_CW = 256
TM = 512
SB = 16
ATT_BLOCKS = 4
ATT_ROWS = ATT_BLOCKS * N_KV_HEADS * 2 * CHUNK
VMEM_LIMIT = 60 * 1024 * 1024


def _np_bucket(rel):
    nb = N_BUCKETS // 2
    max_exact = nb // 2
    ret = np.where(rel > 0, nb, 0)
    n = np.abs(rel)
    nf = np.maximum(n, 1).astype(np.float32)
    large = max_exact + (np.log(nf / np.float32(max_exact)) / np.float32(math.log(MAX_DISTANCE / max_exact))
                         * np.float32(nb - max_exact)).astype(np.int32)
    large = np.minimum(large, nb - 1)
    return (ret + np.where(n < max_exact, n, large)).astype(np.int32)


def _bucket_tables(past_len, dec_seq):
    i = np.arange(CHUNK)[:, None]
    j = np.arange(BAND)[None, :]
    m = np.arange(N_META)[None, :]
    band = _np_bucket(j - WINDOW - i)
    pad = np.full((CHUNK, NKEY - BAND - N_META), -1, np.int32)

    def meta_for(c):
        return _np_bucket(m - (N_META + CHUNK * c + i))

    prompt = []
    for c in range(4):
        b = np.where(c - 2 + j // CHUNK >= 0, band, -1)
        prompt.append(np.concatenate([b, meta_for(c), pad], axis=1))
    assert np.array_equal(meta_for(3), meta_for(4096))
    prompt = np.stack(prompt).astype(np.int32)

    i_s = np.arange(dec_seq)[:, None]
    band_s = np.where(j < WINDOW + dec_seq, _np_bucket(j - WINDOW - i_s), -1)
    meta_s = _np_bucket(m - (N_META + past_len + i_s))
    sample = np.concatenate([band_s, meta_s, pad[:dec_seq]], axis=1)[None].astype(np.int32)

    i_m = np.arange(N_META)[:, None]
    meta_self = np.concatenate([np.full((N_META, BAND), -1, np.int32), _np_bucket(m - i_m), pad[:N_META]],
                               axis=1)[None].astype(np.int32)
    return prompt, sample, meta_self


def _bias_kernel(table_ref, bp_ref, bs_ref, bm_ref, op_ref, os_ref, om_ref):
    for b_ref, o_ref in ((bp_ref, op_ref), (bs_ref, os_ref), (bm_ref, om_ref)):
        nv, r, _ = b_ref.shape
        for v in range(nv):
            bk = b_ref[v]
            for g in range(2):
                for pp in range(2):
                    for hf in range(2):
                        h = 4 * g + 2 * pp + hf
                        acc = jnp.full((r, NKEY), NEG_INF, F32)
                        for b in range(N_BUCKETS):
                            acc = jnp.where(bk == b, table_ref[b, h], acc)
                        o_ref[v, g, r * pp:r * (pp + 1), NKEY * hf:NKEY * (hf + 1)] = acc


def _build_bias(table, bp, bs, bm):
    shapes = [jax.ShapeDtypeStruct((b.shape[0], 2, 2 * b.shape[1], 2 * NKEY), F32) for b in (bp, bs, bm)]
    vm = pl.BlockSpec(memory_space=pltpu.VMEM)
    return pl.pallas_call(
        _bias_kernel,
        out_shape=shapes,
        in_specs=[pl.BlockSpec(memory_space=pltpu.SMEM), vm, vm, vm],
        out_specs=[vm, vm, vm],
        name="rel_bias",
    )(table, bp, bs, bm)


def _dot(a, b):
    return jnp.dot(a, b, preferred_element_type=F32)


def _dot_nt(a, b):
    return lax.dot_general(a, b, (((1,), (1,)), ((), ())), preferred_element_type=F32)


def _rms(x, g):
    ms = jnp.mean(x * x, axis=-1, keepdims=True)
    return (x * lax.rsqrt(ms + EPS)) * g


def _lohi(x):
    lo = lax.broadcasted_iota(jnp.int32, x.shape, 1) < HEAD_DIM
    xr = pltpu.roll(x, HEAD_DIM, axis=1)
    zero = jnp.zeros_like(x)
    return (jnp.where(lo, x, zero).astype(BF16), jnp.where(lo, zero, xr).astype(BF16),
            jnp.where(lo, xr, zero).astype(BF16), jnp.where(lo, zero, x).astype(BF16))


def _attention(nblk, r, get_q, get_k, get_v, get_bias, sinks_ref, s_ref, p_ref, put_out, unit):
    blk = 2 * r
    for b in range(nblk):
        q = get_q(b) * SCALE
        k_parts = get_k(b)
        for g in range(N_KV_HEADS):
            qg = jnp.concatenate([q[:, 256 * g:256 * g + 128], q[:, 256 * g + 128:256 * g + 256]],
                                 axis=0).astype(BF16)
            kk = jnp.concatenate([k_parts[2 * g], k_parts[2 * g + 1]], axis=0)
            row0 = (2 * b + g) * blk
            s_ref[row0:row0 + blk, :] = _dot_nt(qg, kk) + get_bias(b, g)
    total = nblk * N_KV_HEADS * blk
    unit = min(unit, total)
    for u0 in range(0, total, unit):
        for hf in range(2):
            pieces = []
            for j in range(u0 // r, (u0 + unit) // r):
                g, pp = (j // 2) % N_KV_HEADS, j % 2
                pieces.append(jnp.full((r, 1), sinks_ref[4 * g + 2 * pp + hf], F32))
            sink = jnp.concatenate(pieces, axis=0)
            seg = s_ref[u0:u0 + unit, NKEY * hf:NKEY * (hf + 1)]
            mx = jnp.maximum(jnp.max(seg, axis=-1, keepdims=True), sink)
            e = jnp.exp(seg - mx)
            den = jnp.sum(e, axis=-1, keepdims=True) + jnp.exp(sink - mx)
            p_ref[u0:u0 + unit, NKEY * hf:NKEY * (hf + 1)] = (e * (1.0 / den)).astype(BF16)
    for b in range(nblk):
        v_parts = get_v(b)
        for g in range(N_KV_HEADS):
            vv = jnp.concatenate([v_parts[2 * g], v_parts[2 * g + 1]], axis=0)
            row0 = (2 * b + g) * blk
            put_out(b, g, _dot(p_ref[row0:row0 + blk, :], vv))


def _put_heads(mix_ref, row0, r):
    def put(b, g, o):
        base = D_CONV + 256 * g
        mix_ref[row0(b):row0(b) + r, base:base + 128] = o[:r].astype(BF16)
        mix_ref[row0(b):row0(b) + r, base + 128:base + 256] = o[r:].astype(BF16)
    return put


def _in_proj(x_ref, gmix_ref, w_in_ref, hb_ref, z_ref, ns, l, rc):
    g = gmix_ref[...]
    for s in range(ns):
        for r0 in range(0, l, rc):
            hb_ref[s * l + r0:s * l + r0 + rc, :] = _rms(x_ref[s, r0:r0 + rc, :], g).astype(BF16)
    z_ref[...] = _dot(hb_ref[...], w_in_ref[...])


def _delay(x):
    return pltpu.roll(x, 1, axis=0)


def _conv31_block(glu_buf, u0, rc, cw_ref, l0, n):
    nb = (rc + 8) // 8
    acc = None
    for r in range(7, -1, -1):
        s = None
        for a in range(4):
            d = 8 * a + r
            if d >= CONV_WIDTH:
                continue
            blk = glu_buf[u0 - 8 - 8 * a:u0 + rc - 8 * a, l0:l0 + n].reshape(nb, 8, n)
            term = blk * cw_ref[d, :, l0:l0 + n][None]
            s = term if s is None else s + term
        s = s.reshape(rc + 8, n)
        acc = s if acc is None else s + _delay(acc)
    return acc[8:]


def _conv_branch(z_ref, row0, l, glu_buf, cw_ref, dw_b_ref, ln_g_ref, ln_b_ref, mix_ref, rc):
    for r0 in range(0, l, rc):
        a = z_ref[row0 + r0:row0 + r0 + rc, 0:D_CONV]
        b = z_ref[row0 + r0:row0 + r0 + rc, D_CONV:2 * D_CONV]
        glu_buf[HALO + r0:HALO + r0 + rc, :] = a * jax.nn.sigmoid(b)
    for r0 in range(0, l, rc):
        for l0 in range(0, D_CONV, 128):
            z_ref[row0 + r0:row0 + r0 + rc, l0:l0 + 128] = _conv31_block(glu_buf, HALO + r0, rc, cw_ref, l0, 128)
    bias = dw_b_ref[...]
    ln_g = ln_g_ref[...]
    ln_b = ln_b_ref[...]
    rn = min(l, 64)
    for r0 in range(0, l, rn):
        acc = z_ref[row0 + r0:row0 + r0 + rn, 0:D_CONV] + bias
        mu = jnp.mean(acc, axis=-1, keepdims=True)
        d = acc - mu
        var = jnp.mean(d * d, axis=-1, keepdims=True)
        yn = (d * lax.rsqrt(var + EPS)) * ln_g + ln_b
        mix_ref[row0 + r0:row0 + r0 + rn, 0:D_CONV] = (yn * jax.nn.sigmoid(yn)).astype(BF16)


def _out_proj(x_ref, mix_ref, w_out_ref, gffn_ref, x1_ref, hb_ref, ns, l, rc):
    g = gffn_ref[...]
    for s in range(ns):
        for r0 in range(0, l, rc):
            rows = slice(s * l + r0, s * l + r0 + rc)
            x1 = x_ref[s, r0:r0 + rc, :] + _dot(mix_ref[rows, :], w_out_ref[...])
            x1_ref[rows, :] = x1
            hb_ref[rows, :] = _rms(x1, g).astype(BF16)


def _ffn_hidden(hb_ref, w_up_ref, fw_ref, fb_ref, gbuf, ubuf, y_ref, put_halo, save_carry, ns, l, rc):
    nch = D_FF // FFN_CW
    nb = (rc + 8) // 8

    def matmuls(ci):
        c = ci * FFN_CW
        gb = gbuf.at[ci % 2]
        ug = _dot(hb_ref[...], w_up_ref[:, c:c + FFN_CW])
        for s in range(ns):
            put_halo(gb, s, slice(c, c + FFN_CW))
            gb[s, 8:8 + l, :] = ug[s * l:(s + 1) * l]
        ubuf[ci % 2] = _dot(hb_ref[...], w_up_ref[:, D_FF + c:D_FF + c + FFN_CW])

    def elementwise(ci):
        c = ci * FFN_CW
        gb = gbuf.at[ci % 2]
        ub = ubuf.at[ci % 2]
        cols = slice(c, c + FFN_CW)
        w0 = fw_ref[0, :, cols][None]
        w1 = fw_ref[1, :, cols][None]
        w2 = fw_ref[2, :, cols][None]
        bb = fb_ref[:, cols][None]
        for s in range(ns):
            for r0 in range(0, l, rc):
                g = gb[s, r0:r0 + rc + 8, :].reshape(nb, 8, FFN_CW)
                t = (g * w0).reshape(rc + 8, FFN_CW)
                t = (g * w1).reshape(rc + 8, FFN_CW) + _delay(t)
                t = (g * w2 + bb).reshape(rc + 8, FFN_CW) + _delay(t)
                cv = t[8:]
                y = (cv * jax.nn.sigmoid(cv)) * ub[s * l + r0:s * l + r0 + rc, :]
                y_ref[s * l + r0:s * l + r0 + rc, cols] = y.astype(BF16)
            save_carry(gb, s, cols)

    matmuls(0)
    for ci in range(nch):
        if ci + 1 < nch:
            matmuls(ci + 1)
        elementwise(ci)


def _ffn_out(x1_ref, y_ref, w_down_ref, gfin_ref, o_ref, ns, l, rc):
    g = gfin_ref[...]
    for s in range(ns):
        for r0 in range(0, l, rc):
            rows = slice(s * l + r0, s * l + r0 + rc)
            x2 = x1_ref[rows, :] + _dot(y_ref[rows, :], w_down_ref[...])
            o_ref[s, r0:r0 + rc, :] = _rms(x2, g)


def _meta_kernel(sinks_ref, x_ref, bias_ref, gmix_ref, w_in_ref, cw_ref, dw_b_ref, ln_g_ref, ln_b_ref,
                 w_out_ref, gffn_ref, w_upg_ref,
                 glu_o, k_o, v_o, gate_o,
                 hb_ref, z_ref, glu_buf, mix_ref, s_ref, p_ref):
    l = N_META
    _in_proj(x_ref, gmix_ref, w_in_ref, hb_ref, z_ref, 1, l, l)
    glu_buf[0:HALO, :] = jnp.zeros((HALO, D_CONV), F32)
    _conv_branch(z_ref, 0, l, glu_buf, cw_ref, dw_b_ref, ln_g_ref, ln_b_ref, mix_ref, l)
    glu_o[...] = glu_buf[HALO:HALO + l, :]
    k = z_ref[:, _K0:_K0 + D_KV]
    v = z_ref[:, _V0:_V0 + D_KV]
    k_o[...] = k
    v_o[...] = v
    zb = jnp.zeros((BAND, D_KV), BF16)
    zp = jnp.zeros((NKEY - BAND - N_META, D_KV), BF16)
    k_parts = [jnp.concatenate([zb, p, zp], axis=0) for p in _lohi(k)]
    v_parts = [jnp.concatenate([zb, p, zp], axis=0) for p in _lohi(v)]
    _attention(1, l, lambda b: z_ref[:, _Q0:_Q0 + D_ATTN], lambda b: k_parts, lambda b: v_parts,
               lambda b, g: bias_ref[0, g], sinks_ref, s_ref, p_ref, _put_heads(mix_ref, lambda b: 0, l), 4 * l)
    x1 = x_ref[0] + _dot(mix_ref[...], w_out_ref[...])
    h2 = _rms(x1, gffn_ref[...]).astype(BF16)
    gate = _dot(h2, w_upg_ref[...])
    gate_o[...] = gate[N_META - 8:N_META, :]


def _run_meta(sinks, meta_tokens, bias_m, gmix, w_in, cw, dw_b, ln_g, ln_b, w_out, gffn, w_up):
    l = N_META
    full = lambda a: pl.BlockSpec(a.shape, lambda i: (0,) * a.ndim)
    x = meta_tokens[None]
    ins = [x, bias_m, gmix, w_in, cw, dw_b, ln_g, ln_b, w_out, gffn]
    in_specs = [pl.BlockSpec(memory_space=pltpu.SMEM)] + [full(a) for a in ins]
    in_specs.append(pl.BlockSpec((D_MODEL, D_FF), lambda i: (0, 0)))
    out_shape = [jax.ShapeDtypeStruct((l, D_CONV), F32), jax.ShapeDtypeStruct((l, D_KV), F32),
                 jax.ShapeDtypeStruct((l, D_KV), F32), jax.ShapeDtypeStruct((8, D_FF), F32)]
    return pl.pallas_call(
        _meta_kernel,
        grid=(1,),
        out_shape=out_shape,
        in_specs=in_specs,
        out_specs=[pl.BlockSpec(s.shape, lambda i: (0, 0)) for s in out_shape],
        scratch_shapes=[pltpu.VMEM((l, D_MODEL), BF16), pltpu.VMEM((l, D_IN), F32),
                        pltpu.VMEM((HALO + l, D_CONV), F32), pltpu.VMEM((l, D_MODEL), BF16),
                        pltpu.VMEM((4 * l, 2 * NKEY), F32), pltpu.VMEM((4 * l, 2 * NKEY), BF16)],
        compiler_params=pltpu.CompilerParams(vmem_limit_bytes=VMEM_LIMIT),
        name="meta_tokens",
    )(sinks, *ins, w_up)


def _prompt_kernel(sinks_ref, x_ref, glu_m_ref, k_m_ref, v_m_ref, gate_m_ref, bias_ref,
                   gmix_ref, w_in_ref, cw_ref, dw_b_ref, ln_g_ref, ln_b_ref, w_out_ref, gffn_ref,
                   w_up_ref, fw_ref, fb_ref, w_down_ref, gfin_ref,
                   y_o, kwin_o, vwin_o, conva_o, ffn_o,
                   hb_ref, z_ref, glu_buf, kv_buf, mix_ref, x1_ref, gbuf, ubuf, yh_ref, gcarry, s_ref, p_ref):
    t = pl.program_id(1)
    l = TM
    rc = 128

    @pl.when(t == 0)
    def _():
        glu_buf[0:HALO - N_META, :] = jnp.zeros((HALO - N_META, D_CONV), F32)
        glu_buf[HALO - N_META:HALO, :] = glu_m_ref[...]
        kwin_o[0] = jnp.zeros((WINDOW, D_KV), F32)
        vwin_o[0] = jnp.zeros((WINDOW, D_KV), F32)
        gcarry[...] = gate_m_ref[...]

    _in_proj(x_ref, gmix_ref, w_in_ref, hb_ref, z_ref, 1, l, rc)

    _conv_branch(z_ref, 0, l, glu_buf, cw_ref, dw_b_ref, ln_g_ref, ln_b_ref, mix_ref, rc)
    conva_o[0] = glu_buf[HALO + l - (CONV_WIDTH - 1):HALO + l, :]
    glu_buf[0:HALO, :] = glu_buf[l:l + HALO, :]

    for idx, (win_o, c0) in enumerate(((kwin_o, _K0), (vwin_o, _V0))):
        for i, p in enumerate(_lohi(win_o[0])):
            kv_buf[4 * idx + i, 0:WINDOW, :] = p
        for r0 in range(0, l, rc):
            for i, p in enumerate(_lohi(z_ref[r0:r0 + rc, c0:c0 + D_KV])):
                kv_buf[4 * idx + i, WINDOW + r0:WINDOW + r0 + rc, :] = p
        win_o[0] = z_ref[l - WINDOW:l, c0:c0 + D_KV]
    zp = jnp.zeros((NKEY - BAND - N_META, D_KV), BF16)
    km = [jnp.concatenate([p, zp], axis=0) for p in _lohi(k_m_ref[...])]
    vm = [jnp.concatenate([p, zp], axis=0) for p in _lohi(v_m_ref[...])]

    def variant(c):
        return jnp.where(t == 0, c, 3) if c < 3 else 3

    for c0 in range(0, l // CHUNK, ATT_BLOCKS):
        rows = lambda b, c0=c0: (c0 + b) * CHUNK
        _attention(
            ATT_BLOCKS, CHUNK,
            lambda b: z_ref[rows(b):rows(b) + CHUNK, _Q0:_Q0 + D_ATTN],
            lambda b: [jnp.concatenate([kv_buf[i, rows(b):rows(b) + BAND, :], km[i]], axis=0) for i in range(4)],
            lambda b: [jnp.concatenate([kv_buf[4 + i, rows(b):rows(b) + BAND, :], vm[i]], axis=0) for i in range(4)],
            lambda b, g, c0=c0: bias_ref[variant(c0 + b), g],
            sinks_ref, s_ref, p_ref, _put_heads(mix_ref, rows, CHUNK), 512)

    _out_proj(x_ref, mix_ref, w_out_ref, gffn_ref, x1_ref, hb_ref, 1, l, 256)

    def put_halo(gb, s, cols):
        gb[s, 0:8, :] = gcarry[:, cols]

    def save_carry(gb, s, cols):
        gcarry[:, cols] = gb[s, l:l + 8, :]

    _ffn_hidden(hb_ref, w_up_ref, fw_ref, fb_ref, gbuf, ubuf, yh_ref, put_halo, save_carry, 1, l, rc)
    ffn_o[0] = gcarry[8 - (FFN_CONV_WIDTH - 1):8, :]
    _ffn_out(x1_ref, yh_ref, w_down_ref, gfin_ref, y_o, 1, l, 256)


def _const_spec(a):
    return pl.BlockSpec(a.shape, lambda *_: (0,) * a.ndim, pipeline_mode=pl.Buffered(1))


def _run_prompt(sinks, x, glu_m, k_m, v_m, gate_m, bias_p, weights):
    bsz, seq, _ = x.shape
    assert seq % TM == 0 and TM % CHUNK == 0 and TM >= WINDOW
    consts = [glu_m, k_m, v_m, gate_m, bias_p, *weights]
    in_specs = ([pl.BlockSpec(memory_space=pltpu.SMEM), pl.BlockSpec((1, TM, D_MODEL), lambda b, t: (b, t, 0))]
                + [_const_spec(a) for a in consts])
    out_shape = [jax.ShapeDtypeStruct((bsz, seq, D_MODEL), F32),
                 jax.ShapeDtypeStruct((bsz, WINDOW, D_KV), F32), jax.ShapeDtypeStruct((bsz, WINDOW, D_KV), F32),
                 jax.ShapeDtypeStruct((bsz, CONV_WIDTH - 1, D_CONV), F32),
                 jax.ShapeDtypeStruct((bsz, FFN_CONV_WIDTH - 1, D_FF), F32)]
    out_specs = [pl.BlockSpec((1, TM, D_MODEL), lambda b, t: (b, t, 0))] + [
        pl.BlockSpec((1,) + s.shape[1:], lambda b, t: (b, 0, 0)) for s in out_shape[1:]]
    scratch = [pltpu.VMEM((TM, D_MODEL), BF16), pltpu.VMEM((TM, D_IN), F32),
               pltpu.VMEM((HALO + TM, D_CONV), F32), pltpu.VMEM((8, WINDOW + TM, D_KV), BF16),
               pltpu.VMEM((TM, D_MODEL), BF16), pltpu.VMEM((TM, D_MODEL), F32),
               pltpu.VMEM((2, 1, 8 + TM, FFN_CW), F32), pltpu.VMEM((2, TM, FFN_CW), F32),
               pltpu.VMEM((TM, D_FF), BF16), pltpu.VMEM((8, D_FF), F32),
               pltpu.VMEM((ATT_ROWS, 2 * NKEY), F32), pltpu.VMEM((ATT_ROWS, 2 * NKEY), BF16)]
    return pl.pallas_call(
        _prompt_kernel,
        grid=(bsz, seq // TM),
        out_shape=out_shape,
        in_specs=in_specs,
        out_specs=out_specs,
        scratch_shapes=scratch,
        compiler_params=pltpu.CompilerParams(dimension_semantics=("arbitrary", "arbitrary"),
                                             vmem_limit_bytes=VMEM_LIMIT),
        name="prompt_step",
    )(sinks, x, *consts)


def _sample_kernel(sinks_ref, x_ref, kmeta_ref, vmeta_ref, kwin_ref, vwin_ref, conva_ref, ffn_ref, bias_ref,
                   gmix_ref, w_in_ref, cw_ref, dw_b_ref, ln_g_ref, ln_b_ref, w_out_ref, gffn_ref,
                   w_up_ref, fw_ref, fb_ref, w_down_ref, gfin_ref,
                   y_o, kwin_o, vwin_o, conva_o, ffn_o,
                   hb_ref, z_ref, glu_buf, mix_ref, x1_ref, gbuf, ubuf, yh_ref, s_ref, p_ref):
    ns, l, _ = x_ref.shape
    _in_proj(x_ref, gmix_ref, w_in_ref, hb_ref, z_ref, ns, l, l)
    glu_buf[0:HALO - 30, :] = jnp.zeros((HALO - 30, D_CONV), F32)
    zgap = jnp.zeros((BAND - WINDOW - l, D_KV), BF16)
    zp = jnp.zeros((NKEY - BAND - N_META, D_KV), BF16)
    for s in range(ns):
        row0 = s * l
        glu_buf[HALO - (CONV_WIDTH - 1):HALO, :] = conva_ref[s]
        _conv_branch(z_ref, row0, l, glu_buf, cw_ref, dw_b_ref, ln_g_ref, ln_b_ref, mix_ref, l)
        conva_o[s] = glu_buf[HALO + l - (CONV_WIDTH - 1):HALO + l, :]
        for win_ref, win_o, c0 in ((kwin_ref, kwin_o, _K0), (vwin_ref, vwin_o, _V0)):
            win_o[s, 0:WINDOW - l, :] = win_ref[s, l:WINDOW, :]
            win_o[s, WINDOW - l:WINDOW, :] = z_ref[row0:row0 + l, c0:c0 + D_KV]

    def key_slots(win_ref, meta_ref, c0):
        def get(s):
            new = z_ref[s * l:(s + 1) * l, c0:c0 + D_KV]
            return [jnp.concatenate([pw, pn, zgap, pm, zp], axis=0)
                    for pw, pn, pm in zip(_lohi(win_ref[s]), _lohi(new), _lohi(meta_ref[s]))]
        return get

    nb = ATT_ROWS // (N_KV_HEADS * 2 * l)
    for s0 in range(0, ns, nb):
        rows = lambda b, s0=s0: (s0 + b) * l
        get_k = key_slots(kwin_ref, kmeta_ref, _K0)
        get_v = key_slots(vwin_ref, vmeta_ref, _V0)
        _attention(nb, l, lambda b: z_ref[rows(b):rows(b) + l, _Q0:_Q0 + D_ATTN],
                   lambda b, s0=s0: get_k(s0 + b), lambda b, s0=s0: get_v(s0 + b),
                   lambda b, g: bias_ref[0, g], sinks_ref, s_ref, p_ref, _put_heads(mix_ref, rows, l), 512)

    _out_proj(x_ref, mix_ref, w_out_ref, gffn_ref, x1_ref, hb_ref, ns, l, l)

    nprev = FFN_CONV_WIDTH - 1
    for i in range(2):
        for s in range(ns):
            gbuf[i, s, 0:8, :] = jnp.zeros((8, FFN_CW), F32)

    def put_halo(gb, s, cols):
        gb[s, 8 - nprev:8, :] = ffn_ref[s, :, cols]

    def save_carry(gb, s, cols):
        ffn_o[s, :, cols] = gb[s, 8 + l - nprev:8 + l, :]

    _ffn_hidden(hb_ref, w_up_ref, fw_ref, fb_ref, gbuf, ubuf, yh_ref, put_halo, save_carry, ns, l, l)
    _ffn_out(x1_ref, yh_ref, w_down_ref, gfin_ref, y_o, ns, l, l)


def _run_sample(sinks, x, kmeta, vmeta, kwin, vwin, conva, ffn, bias_s, weights):
    nstream, l, _ = x.shape
    assert nstream % SB == 0 and l % 16 == 0 and l <= CHUNK and kwin.shape[1] == WINDOW
    rows = SB * l
    per_stream = [x, kmeta, vmeta, kwin, vwin, conva, ffn]
    consts = [bias_s, *weights]
    in_specs = ([pl.BlockSpec(memory_space=pltpu.SMEM)]
                + [pl.BlockSpec((SB,) + a.shape[1:], lambda i: (i, 0, 0)) for a in per_stream]
                + [_const_spec(a) for a in consts])
    out_shape = [jax.ShapeDtypeStruct(a.shape, F32) for a in (x, kwin, vwin, conva, ffn)]
    out_specs = [pl.BlockSpec((SB,) + s.shape[1:], lambda i: (i, 0, 0)) for s in out_shape]
    scratch = [pltpu.VMEM((rows, D_MODEL), BF16), pltpu.VMEM((rows, D_IN), F32),
               pltpu.VMEM((HALO + l, D_CONV), F32),
               pltpu.VMEM((rows, D_MODEL), BF16), pltpu.VMEM((rows, D_MODEL), F32),
               pltpu.VMEM((2, SB, 8 + l, FFN_CW), F32), pltpu.VMEM((2, rows, FFN_CW), F32),
               pltpu.VMEM((rows, D_FF), BF16),
               pltpu.VMEM((ATT_ROWS, 2 * NKEY), F32), pltpu.VMEM((ATT_ROWS, 2 * NKEY), BF16)]
    return pl.pallas_call(
        _sample_kernel,
        grid=(nstream // SB,),
        out_shape=out_shape,
        in_specs=in_specs,
        out_specs=out_specs,
        scratch_shapes=scratch,
        compiler_params=pltpu.CompilerParams(dimension_semantics=("arbitrary",), vmem_limit_bytes=VMEM_LIMIT),
        name="sample_step",
    )(sinks, *per_stream, *consts)


def kernel(x_prompt, x_sample, cache_k_meta, cache_v_meta, cache_k_win, cache_v_win, state_conv_a, state_conv_ffn,
           meta_tokens, rel_bias_table, norm_mix, w_in, conv_dw_w, conv_dw_b, conv_ln_g, conv_ln_b, attn_sinks,
           w_out, norm_ffn, w_up, ffn_dw_w, ffn_dw_b, w_down, norm_final):
    depth = w_in.shape[0]
    assert depth == 1, "single-layer step"
    bsz = x_prompt.shape[0]
    nstream, dec_seq, _ = x_sample.shape
    win = cache_k_win.shape[2]
    assert win == WINDOW

    bp, bs, bm = _bucket_tables(PAST_LEN, dec_seq)
    bias_p, bias_s, bias_m = _build_bias(rel_bias_table, jnp.asarray(bp), jnp.asarray(bs), jnp.asarray(bm))

    row = lambda a: a.reshape(1, -1)
    w_in_b = w_in[0].astype(BF16)
    w_out_b = w_out[0].astype(BF16)
    w_up_b = w_up[0].astype(BF16)
    w_down_b = w_down[0].astype(BF16)
    sinks = attn_sinks[0]
    gmix, gffn, gfin = row(norm_mix[0]), row(norm_ffn[0]), row(norm_final)
    cw = jnp.broadcast_to(conv_dw_w[0][::-1][:, None, :], (CONV_WIDTH, 8, D_CONV))
    dw_b = row(conv_dw_b[0])
    ln_g, ln_b = row(conv_ln_g[0]), row(conv_ln_b[0])
    fw = jnp.broadcast_to(ffn_dw_w[0][:, None, :], (FFN_CONV_WIDTH, 8, D_FF))
    fb = jnp.broadcast_to(ffn_dw_b[0][None, :], (8, D_FF))

    glu_m, k_m, v_m, gate_m = _run_meta(sinks, meta_tokens, bias_m, gmix, w_in_b, cw, dw_b, ln_g, ln_b,
                                        w_out_b, gffn, w_up_b)

    weights = [gmix, w_in_b, cw, dw_b, ln_g, ln_b, w_out_b, gffn, w_up_b, fw, fb, w_down_b, gfin]
    y_p, kwin_p, vwin_p, conva_p, ffn_p = _run_prompt(sinks, x_prompt, glu_m, k_m, v_m, gate_m, bias_p, weights)

    kv = lambda a: a.reshape(a.shape[0], a.shape[1], D_KV)
    y_s, kwin_s, vwin_s, conva_s, ffn_s = _run_sample(
        sinks, x_sample, kv(cache_k_meta[0]), kv(cache_v_meta[0]), kv(cache_k_win[0]), kv(cache_v_win[0]),
        state_conv_a[0], state_conv_ffn[0], bias_s, weights)

    heads = lambda a: a.reshape(1, a.shape[0], a.shape[1], N_KV_HEADS, HEAD_DIM)
    meta_b = lambda a: jnp.broadcast_to(a.reshape(1, 1, N_META, N_KV_HEADS, HEAD_DIM),
                                        (1, bsz, N_META, N_KV_HEADS, HEAD_DIM))
    return (y_p, y_s, meta_b(k_m), meta_b(v_m), heads(kwin_p), heads(vwin_p), conva_p[None], ffn_p[None],
            heads(kwin_s), heads(vwin_s), conva_s[None], ffn_s[None])
```

```python
import functools
import math

import numpy as np
import jax
import jax.numpy as jnp
from jax import lax
from jax.experimental import pallas as pl
from jax.experimental.pallas import tpu as pltpu

F32 = jnp.float32
BF16 = jnp.bfloat16

D_MODEL = 1024
CHUNK = 64
N_META = 16
D_CONV = 512
CONV_WIDTH = 31
N_HEADS = 8
N_KV_HEADS = 2
HEAD_DIM = 64
D_ATTN = N_HEADS * HEAD_DIM
D_KV = N_KV_HEADS * HEAD_DIM
WINDOW = 128
PAST_LEN = 4096
N_BUCKETS = 32
MAX_DISTANCE = 256
D_FF = 2816
FFN_CONV_WIDTH = 3
D_IN = 2 * D_CONV + D_ATTN + 2 * D_KV
EPS = 1e-6
NEG_INF = -1e30
SCALE = HEAD_DIM ** -0.5

_Q0 = 2 * D_CONV
_K0 = _Q0 + D_ATTN
_V0 = _K0 + D_KV

NKEY = 256
BAND = WINDOW + CHUNK
HALO = 32
FFN_CW = 256
TM = 512
SB = 16
VMEM_LIMIT = 60 * 1024 * 1024


def _np_bucket(rel):
    nb = N_BUCKETS // 2
    max_exact = nb // 2
    ret = np.where(rel > 0, nb, 0)
    n = np.abs(rel)
    nf = np.maximum(n, 1).astype(np.float32)
    large = max_exact + (np.log(nf / np.float32(max_exact)) / np.float32(math.log(MAX_DISTANCE / max_exact))
                         * np.float32(nb - max_exact)).astype(np.int32)
    large = np.minimum(large, nb - 1)
    return (ret + np.where(n < max_exact, n, large)).astype(np.int32)


def _bucket_tables(past_len, dec_seq):
    i = np.arange(CHUNK)[:, None]
    j = np.arange(BAND)[None, :]
    m = np.arange(N_META)[None, :]
    band = _np_bucket(j - WINDOW - i)
    pad = np.full((CHUNK, NKEY - BAND - N_META), -1, np.int32)

    def meta_for(c):
        return _np_bucket(m - (N_META + CHUNK * c + i))

    prompt = []
    for c in range(4):
        b = np.where(c - 2 + j // CHUNK >= 0, band, -1)
        prompt.append(np.concatenate([b, meta_for(c), pad], axis=1))
    assert np.array_equal(meta_for(3), meta_for(4096))
    prompt = np.stack(prompt).astype(np.int32)

    i_s = np.arange(dec_seq)[:, None]
    band_s = np.where(j < WINDOW + dec_seq, _np_bucket(j - WINDOW - i_s), -1)
    meta_s = _np_bucket(m - (N_META + past_len + i_s))
    sample = np.concatenate([band_s, meta_s, pad[:dec_seq]], axis=1)[None].astype(np.int32)

    i_m = np.arange(N_META)[:, None]
    meta_self = np.concatenate([np.full((N_META, BAND), -1, np.int32), _np_bucket(m - i_m), pad[:N_META]],
                               axis=1)[None].astype(np.int32)
    return prompt, sample, meta_self


def _bias_kernel(table_ref, bp_ref, bs_ref, bm_ref, op_ref, os_ref, om_ref):
    for b_ref, o_ref in ((bp_ref, op_ref), (bs_ref, os_ref), (bm_ref, om_ref)):
        nv, r, _ = b_ref.shape
        for v in range(nv):
            bk = b_ref[v]
            for g in range(2):
                for pp in range(2):
                    for hf in range(2):
                        h = 4 * g + 2 * pp + hf
                        acc = jnp.full((r, NKEY), NEG_INF, F32)
                        for b in range(N_BUCKETS):
                            acc = jnp.where(bk == b, table_ref[b, h], acc)
                        o_ref[v, g, r * pp:r * (pp + 1), NKEY * hf:NKEY * (hf + 1)] = acc


def _build_bias(table, bp, bs, bm):
    shapes = [jax.ShapeDtypeStruct((b.shape[0], 2, 2 * b.shape[1], 2 * NKEY), F32) for b in (bp, bs, bm)]
    vm = pl.BlockSpec(memory_space=pltpu.VMEM)
    return pl.pallas_call(
        _bias_kernel,
        out_shape=shapes,
        in_specs=[pl.BlockSpec(memory_space=pltpu.SMEM), vm, vm, vm],
        out_specs=[vm, vm, vm],
        name="rel_bias",
    )(table, bp, bs, bm)


def _dot(a, b):
    return jnp.dot(a, b, preferred_element_type=F32)


def _dot_nt(a, b):
    return lax.dot_general(a, b, (((1,), (1,)), ((), ())), preferred_element_type=F32)


def _rms(x, g):
    ms = jnp.mean(x * x, axis=-1, keepdims=True)
    return (x * lax.rsqrt(ms + EPS)) * g


def _lohi(x):
    lo = lax.broadcasted_iota(jnp.int32, x.shape, 1) < HEAD_DIM
    xr = pltpu.roll(x, HEAD_DIM, axis=1)
    zero = jnp.zeros_like(x)
    return (jnp.where(lo, x, zero).astype(BF16), jnp.where(lo, zero, xr).astype(BF16),
            jnp.where(lo, xr, zero).astype(BF16), jnp.where(lo, zero, x).astype(BF16))


def _attention(nblk, r, get_q, get_k, get_v, get_bias, sinks_ref, put_out):
    for b in range(nblk):
        q = get_q(b) * SCALE
        k_parts = get_k(b)
        v_parts = get_v(b)
        for g in range(N_KV_HEADS):
            qg = jnp.concatenate([q[:, 256 * g:256 * g + 128], q[:, 256 * g + 128:256 * g + 256]],
                                 axis=0).astype(BF16)
            kk = jnp.concatenate([k_parts[2 * g], k_parts[2 * g + 1]], axis=0)
            vv = jnp.concatenate([v_parts[2 * g], v_parts[2 * g + 1]], axis=0)
            s = _dot_nt(qg, kk) + get_bias(b, g)
            rows = []
            for pp in range(2):
                segs = []
                for hf in range(2):
                    sink = sinks_ref[4 * g + 2 * pp + hf]
                    seg = s[r * pp:r * (pp + 1), NKEY * hf:NKEY * (hf + 1)]
                    mx = jnp.maximum(jnp.max(seg, axis=-1, keepdims=True), sink)
                    e = jnp.exp(seg - mx)
                    den = jnp.sum(e, axis=-1, keepdims=True) + jnp.exp(sink - mx)
                    segs.append((e * (1.0 / den)).astype(BF16))
                rows.append(jnp.concatenate(segs, axis=1))
            put_out(b, g, _dot(jnp.concatenate(rows, axis=0), vv))


def _put_heads(mix_ref, row0, r):
    def put(b, g, o):
        base = D_CONV + 256 * g
        mix_ref[row0(b):row0(b) + r, base:base + 128] = o[:r].astype(BF16)
        mix_ref[row0(b):row0(b) + r, base + 128:base + 256] = o[r:].astype(BF16)
    return put


def _in_proj(x_ref, gmix_ref, w_in_ref, hb_ref, z_ref, ns, l, rc):
    g = gmix_ref[...]
    for s in range(ns):
        for r0 in range(0, l, rc):
            hb_ref[s * l + r0:s * l + r0 + rc, :] = _rms(x_ref[s, r0:r0 + rc, :], g).astype(BF16)
    z_ref[...] = _dot(hb_ref[...], w_in_ref[...])


def _delay(x):
    return pltpu.roll(x, 1, axis=0)


def _conv31_block(glu_buf, u0, rc, cw_ref, l0, n):
    nb = (rc + 8) // 8
    acc = None
    for r in range(7, -1, -1):
        s = None
        for a in range(4):
            d = 8 * a + r
            if d >= CONV_WIDTH:
                continue
            blk = glu_buf[u0 - 8 - 8 * a:u0 + rc - 8 * a, l0:l0 + n].reshape(nb, 8, n)
            term = blk * cw_ref[d, :, l0:l0 + n][None]
            s = term if s is None else s + term
        s = s.reshape(rc + 8, n)
        acc = s if acc is None else s + _delay(acc)
    return acc[8:]


def _conv_branch(z_ref, row0, l, glu_buf, cw_ref, dw_b_ref, ln_g_ref, ln_b_ref, mix_ref, rc):
    for r0 in range(0, l, rc):
        a = z_ref[row0 + r0:row0 + r0 + rc, 0:D_CONV]
        b = z_ref[row0 + r0:row0 + r0 + rc, D_CONV:2 * D_CONV]
        glu_buf[HALO + r0:HALO + r0 + rc, :] = a * jax.nn.sigmoid(b)
    for r0 in range(0, l, rc):
        for l0 in range(0, D_CONV, 128):
            z_ref[row0 + r0:row0 + r0 + rc, l0:l0 + 128] = _conv31_block(glu_buf, HALO + r0, rc, cw_ref, l0, 128)
    bias = dw_b_ref[...]
    ln_g = ln_g_ref[...]
    ln_b = ln_b_ref[...]
    rn = min(l, 64)
    for r0 in range(0, l, rn):
        acc = z_ref[row0 + r0:row0 + r0 + rn, 0:D_CONV] + bias
        mu = jnp.mean(acc, axis=-1, keepdims=True)
        d = acc - mu
        var = jnp.mean(d * d, axis=-1, keepdims=True)
        yn = (d * lax.rsqrt(var + EPS)) * ln_g + ln_b
        mix_ref[row0 + r0:row0 + r0 + rn, 0:D_CONV] = (yn * jax.nn.sigmoid(yn)).astype(BF16)


def _seq_pieces(m0, mc, l):
    pc = min(l, mc)
    return [(p0,) + divmod(m0 + p0, l) + (pc,) for p0 in range(0, mc, pc)]


def _out_proj(x_ref, mix_ref, w_out_ref, gffn_ref, x1_ref, hb_ref, ns, l, mc):
    g = gffn_ref[...]
    for m0 in range(0, ns * l, mc):
        acc = _dot(mix_ref[m0:m0 + mc, :], w_out_ref[...])
        for p0, s, r0, pc in _seq_pieces(m0, mc, l):
            x1 = x_ref[s, r0:r0 + pc, :] + acc[p0:p0 + pc]
            x1_ref[m0 + p0:m0 + p0 + pc, :] = x1
            hb_ref[m0 + p0:m0 + p0 + pc, :] = _rms(x1, g).astype(BF16)


def _ffn_hidden(hb_ref, w_up_ref, fw_ref, fb_ref, gbuf, ubuf, y_ref, put_halo, save_carry, ns, l, rc):
    nch = D_FF // FFN_CW
    nb = (rc + 8) // 8

    def matmuls(ci):
        c = ci * FFN_CW
        gb = gbuf.at[ci % 2]
        ug = _dot(hb_ref[...], w_up_ref[:, c:c + FFN_CW])
        for s in range(ns):
            put_halo(gb, s, slice(c, c + FFN_CW))
            gb[s, 8:8 + l, :] = ug[s * l:(s + 1) * l]
        ubuf[ci % 2] = _dot(hb_ref[...], w_up_ref[:, D_FF + c:D_FF + c + FFN_CW])

    def elementwise(ci):
        c = ci * FFN_CW
        gb = gbuf.at[ci % 2]
        ub = ubuf.at[ci % 2]
        cols = slice(c, c + FFN_CW)
        w0 = fw_ref[0, :, cols][None]
        w1 = fw_ref[1, :, cols][None]
        w2 = fw_ref[2, :, cols][None]
        bb = fb_ref[:, cols][None]
        for s in range(ns):
            for r0 in range(0, l, rc):
                g = gb[s, r0:r0 + rc + 8, :].reshape(nb, 8, FFN_CW)
                t = (g * w0).reshape(rc + 8, FFN_CW)
                t = (g * w1).reshape(rc + 8, FFN_CW) + _delay(t)
                t = (g * w2 + bb).reshape(rc + 8, FFN_CW) + _delay(t)
                cv = t[8:]
                y = (cv * jax.nn.sigmoid(cv)) * ub[s * l + r0:s * l + r0 + rc, :]
                y_ref[s * l + r0:s * l + r0 + rc, cols] = y.astype(BF16)
            save_carry(gb, s, cols)

    matmuls(0)
    for ci in range(nch):
        if ci + 1 < nch:
            matmuls(ci + 1)
        elementwise(ci)


def _ffn_out(x1_ref, y_ref, w_down_ref, gfin_ref, o_ref, ns, l, mc):
    g = gfin_ref[...]
    for m0 in range(0, ns * l, mc):
        acc = _dot(y_ref[m0:m0 + mc, :], w_down_ref[...])
        for p0, s, r0, pc in _seq_pieces(m0, mc, l):
            x2 = x1_ref[m0 + p0:m0 + p0 + pc, :] + acc[p0:p0 + pc]
            o_ref[s, r0:r0 + pc, :] = _rms(x2, g)


def _meta_kernel(sinks_ref, x_ref, bias_ref, gmix_ref, w_in_ref, cw_ref, dw_b_ref, ln_g_ref, ln_b_ref,
                 w_out_ref, gffn_ref, w_upg_ref,
                 glu_o, k_o, v_o, gate_o,
                 hb_ref, z_ref, glu_buf, mix_ref):
    l = N_META
    _in_proj(x_ref, gmix_ref, w_in_ref, hb_ref, z_ref, 1, l, l)
    glu_buf[0:HALO, :] = jnp.zeros((HALO, D_CONV), F32)
    _conv_branch(z_ref, 0, l, glu_buf, cw_ref, dw_b_ref, ln_g_ref, ln_b_ref, mix_ref, l)
    glu_o[...] = glu_buf[HALO:HALO + l, :]
    k = z_ref[:, _K0:_K0 + D_KV]
    v = z_ref[:, _V0:_V0 + D_KV]
    k_o[...] = k
    v_o[...] = v
    zb = jnp.zeros((BAND, D_KV), BF16)
    zp = jnp.zeros((NKEY - BAND - N_META, D_KV), BF16)
    k_parts = [jnp.concatenate([zb, p, zp], axis=0) for p in _lohi(k)]
    v_parts = [jnp.concatenate([zb, p, zp], axis=0) for p in _lohi(v)]
    _attention(1, l, lambda b: z_ref[:, _Q0:_Q0 + D_ATTN], lambda b: k_parts, lambda b: v_parts,
               lambda b, g: bias_ref[0, g], sinks_ref, _put_heads(mix_ref, lambda b: 0, l))
    x1 = x_ref[0] + _dot(mix_ref[...], w_out_ref[...])
    h2 = _rms(x1, gffn_ref[...]).astype(BF16)
    gate = _dot(h2, w_upg_ref[...])
    gate_o[...] = gate[N_META - 8:N_META, :]


def _run_meta(sinks, meta_tokens, bias_m, gmix, w_in, cw, dw_b, ln_g, ln_b, w_out, gffn, w_up):
    l = N_META
    full = lambda a: pl.BlockSpec(a.shape, lambda i: (0,) * a.ndim)
    x = meta_tokens[None]
    ins = [x, bias_m, gmix, w_in, cw, dw_b, ln_g, ln_b, w_out, gffn]
    in_specs = [pl.BlockSpec(memory_space=pltpu.SMEM)] + [full(a) for a in ins]
    in_specs.append(pl.BlockSpec((D_MODEL, D_FF), lambda i: (0, 0)))
    out_shape = [jax.ShapeDtypeStruct((l, D_CONV), F32), jax.ShapeDtypeStruct((l, D_KV), F32),
                 jax.ShapeDtypeStruct((l, D_KV), F32), jax.ShapeDtypeStruct((8, D_FF), F32)]
    return pl.pallas_call(
        _meta_kernel,
        grid=(1,),
        out_shape=out_shape,
        in_specs=in_specs,
        out_specs=[pl.BlockSpec(s.shape, lambda i: (0, 0)) for s in out_shape],
        scratch_shapes=[pltpu.VMEM((l, D_MODEL), BF16), pltpu.VMEM((l, D_IN), F32),
                        pltpu.VMEM((HALO + l, D_CONV), F32), pltpu.VMEM((l, D_MODEL), BF16)],
        compiler_params=pltpu.CompilerParams(vmem_limit_bytes=VMEM_LIMIT),
        name="meta_tokens",
    )(sinks, *ins, w_up)


def _prompt_kernel(sinks_ref, x_ref, glu_m_ref, k_m_ref, v_m_ref, gate_m_ref, bias_ref,
                   gmix_ref, w_in_ref, cw_ref, dw_b_ref, ln_g_ref, ln_b_ref, w_out_ref, gffn_ref,
                   w_up_ref, fw_ref, fb_ref, w_down_ref, gfin_ref,
                   y_o, kwin_o, vwin_o, conva_o, ffn_o,
                   hb_ref, z_ref, glu_buf, kv_buf, mix_ref, x1_ref, gbuf, ubuf, yh_ref, gcarry):
    t = pl.program_id(1)
    l = TM
    rc = 128

    @pl.when(t == 0)
    def _():
        glu_buf[0:HALO - N_META, :] = jnp.zeros((HALO - N_META, D_CONV), F32)
        glu_buf[HALO - N_META:HALO, :] = glu_m_ref[...]
        kwin_o[0] = jnp.zeros((WINDOW, D_KV), F32)
        vwin_o[0] = jnp.zeros((WINDOW, D_KV), F32)
        gcarry[...] = gate_m_ref[...]

    _in_proj(x_ref, gmix_ref, w_in_ref, hb_ref, z_ref, 1, l, rc)

    _conv_branch(z_ref, 0, l, glu_buf, cw_ref, dw_b_ref, ln_g_ref, ln_b_ref, mix_ref, rc)
    conva_o[0] = glu_buf[HALO + l - (CONV_WIDTH - 1):HALO + l, :]
    glu_buf[0:HALO, :] = glu_buf[l:l + HALO, :]

    for idx, (win_o, c0) in enumerate(((kwin_o, _K0), (vwin_o, _V0))):
        for i, p in enumerate(_lohi(win_o[0])):
            kv_buf[4 * idx + i, 0:WINDOW, :] = p
        for r0 in range(0, l, rc):
            for i, p in enumerate(_lohi(z_ref[r0:r0 + rc, c0:c0 + D_KV])):
                kv_buf[4 * idx + i, WINDOW + r0:WINDOW + r0 + rc, :] = p
        win_o[0] = z_ref[l - WINDOW:l, c0:c0 + D_KV]
    zp = jnp.zeros((NKEY - BAND - N_META, D_KV), BF16)
    km = [jnp.concatenate([p, zp], axis=0) for p in _lohi(k_m_ref[...])]
    vm = [jnp.concatenate([p, zp], axis=0) for p in _lohi(v_m_ref[...])]

    def variant(c):
        return jnp.where(t == 0, c, 3) if c < 3 else 3

    rows = lambda c: c * CHUNK
    _attention(
        l // CHUNK, CHUNK,
        lambda c: z_ref[rows(c):rows(c) + CHUNK, _Q0:_Q0 + D_ATTN],
        lambda c: [jnp.concatenate([kv_buf[i, rows(c):rows(c) + BAND, :], km[i]], axis=0) for i in range(4)],
        lambda c: [jnp.concatenate([kv_buf[4 + i, rows(c):rows(c) + BAND, :], vm[i]], axis=0) for i in range(4)],
        lambda c, g: bias_ref[variant(c), g],
        sinks_ref, _put_heads(mix_ref, rows, CHUNK))

    _out_proj(x_ref, mix_ref, w_out_ref, gffn_ref, x1_ref, hb_ref, 1, l, 256)

    def put_halo(gb, s, cols):
        gb[s, 0:8, :] = gcarry[:, cols]

    def save_carry(gb, s, cols):
        gcarry[:, cols] = gb[s, l:l + 8, :]

    _ffn_hidden(hb_ref, w_up_ref, fw_ref, fb_ref, gbuf, ubuf, yh_ref, put_halo, save_carry, 1, l, rc)
    ffn_o[0] = gcarry[8 - (FFN_CONV_WIDTH - 1):8, :]
    _ffn_out(x1_ref, yh_ref, w_down_ref, gfin_ref, y_o, 1, l, 256)


def _const_spec(a):
    return pl.BlockSpec(a.shape, lambda *_: (0,) * a.ndim, pipeline_mode=pl.Buffered(1))


def _run_prompt(sinks, x, glu_m, k_m, v_m, gate_m, bias_p, weights):
    bsz, seq, _ = x.shape
    assert seq % TM == 0 and TM % CHUNK == 0 and TM >= WINDOW
    consts = [glu_m, k_m, v_m, gate_m, bias_p, *weights]
    in_specs = ([pl.BlockSpec(memory_space=pltpu.SMEM), pl.BlockSpec((1, TM, D_MODEL), lambda b, t: (b, t, 0))]
                + [_const_spec(a) for a in consts])
    out_shape = [jax.ShapeDtypeStruct((bsz, seq, D_MODEL), F32),
                 jax.ShapeDtypeStruct((bsz, WINDOW, D_KV), F32), jax.ShapeDtypeStruct((bsz, WINDOW, D_KV), F32),
                 jax.ShapeDtypeStruct((bsz, CONV_WIDTH - 1, D_CONV), F32),
                 jax.ShapeDtypeStruct((bsz, FFN_CONV_WIDTH - 1, D_FF), F32)]
    out_specs = [pl.BlockSpec((1, TM, D_MODEL), lambda b, t: (b, t, 0))] + [
        pl.BlockSpec((1,) + s.shape[1:], lambda b, t: (b, 0, 0)) for s in out_shape[1:]]
    scratch = [pltpu.VMEM((TM, D_MODEL), BF16), pltpu.VMEM((TM, D_IN), F32),
               pltpu.VMEM((HALO + TM, D_CONV), F32), pltpu.VMEM((8, WINDOW + TM, D_KV), BF16),
               pltpu.VMEM((TM, D_MODEL), BF16), pltpu.VMEM((TM, D_MODEL), F32),
               pltpu.VMEM((2, 1, 8 + TM, FFN_CW), F32), pltpu.VMEM((2, TM, FFN_CW), F32),
               pltpu.VMEM((TM, D_FF), BF16), pltpu.VMEM((8, D_FF), F32)]
    return pl.pallas_call(
        _prompt_kernel,
        grid=(bsz, seq // TM),
        out_shape=out_shape,
        in_specs=in_specs,
        out_specs=out_specs,
        scratch_shapes=scratch,
        compiler_params=pltpu.CompilerParams(dimension_semantics=("arbitrary", "arbitrary"),
                                             vmem_limit_bytes=VMEM_LIMIT),
        name="prompt_step",
    )(sinks, x, *consts)


def _sample_kernel(sinks_ref, x_ref, kmeta_ref, vmeta_ref, kwin_ref, vwin_ref, conva_ref, ffn_ref, bias_ref,
                   gmix_ref, w_in_ref, cw_ref, dw_b_ref, ln_g_ref, ln_b_ref, w_out_ref, gffn_ref,
                   w_up_ref, fw_ref, fb_ref, w_down_ref, gfin_ref,
                   y_o, kwin_o, vwin_o, conva_o, ffn_o,
                   hb_ref, z_ref, glu_buf, mix_ref, x1_ref, gbuf, ubuf, yh_ref):
    ns, l, _ = x_ref.shape
    _in_proj(x_ref, gmix_ref, w_in_ref, hb_ref, z_ref, ns, l, l)
    glu_buf[0:HALO - 30, :] = jnp.zeros((HALO - 30, D_CONV), F32)
    zgap = jnp.zeros((BAND - WINDOW - l, D_KV), BF16)
    zp = jnp.zeros((NKEY - BAND - N_META, D_KV), BF16)
    for s in range(ns):
        row0 = s * l
        glu_buf[HALO - (CONV_WIDTH - 1):HALO, :] = conva_ref[s]
        _conv_branch(z_ref, row0, l, glu_buf, cw_ref, dw_b_ref, ln_g_ref, ln_b_ref, mix_ref, l)
        conva_o[s] = glu_buf[HALO + l - (CONV_WIDTH - 1):HALO + l, :]
        for win_ref, win_o, c0 in ((kwin_ref, kwin_o, _K0), (vwin_ref, vwin_o, _V0)):
            win_o[s, 0:WINDOW - l, :] = win_ref[s, l:WINDOW, :]
            win_o[s, WINDOW - l:WINDOW, :] = z_ref[row0:row0 + l, c0:c0 + D_KV]

    def key_slots(win_ref, meta_ref, c0):
        def get(s):
            new = z_ref[s * l:(s + 1) * l, c0:c0 + D_KV]
            return [jnp.concatenate([pw, pn, zgap, pm, zp], axis=0)
                    for pw, pn, pm in zip(_lohi(win_ref[s]), _lohi(new), _lohi(meta_ref[s]))]
        return get

    rows = lambda s: s * l
    _attention(ns, l, lambda s: z_ref[rows(s):rows(s) + l, _Q0:_Q0 + D_ATTN],
               key_slots(kwin_ref, kmeta_ref, _K0), key_slots(vwin_ref, vmeta_ref, _V0),
               lambda s, g: bias_ref[0, g], sinks_ref, _put_heads(mix_ref, rows, l))

    _out_proj(x_ref, mix_ref, w_out_ref, gffn_ref, x1_ref, hb_ref, ns, l, 256)

    nprev = FFN_CONV_WIDTH - 1
    for i in range(2):
        for s in range(ns):
            gbuf[i, s, 0:8, :] = jnp.zeros((8, FFN_CW), F32)

    def put_halo(gb, s, cols):
        gb[s, 8 - nprev:8, :] = ffn_ref[s, :, cols]

    def save_carry(gb, s, cols):
        ffn_o[s, :, cols] = gb[s, 8 + l - nprev:8 + l, :]

    _ffn_hidden(hb_ref, w_up_ref, fw_ref, fb_ref, gbuf, ubuf, yh_ref, put_halo, save_carry, ns, l, l)
    _ffn_out(x1_ref, yh_ref, w_down_ref, gfin_ref, y_o, ns, l, 256)


def _run_sample(sinks, x, kmeta, vmeta, kwin, vwin, conva, ffn, bias_s, weights):
    nstream, l, _ = x.shape
    assert nstream % SB == 0 and l % 16 == 0 and l <= CHUNK and kwin.shape[1] == WINDOW
    rows = SB * l
    per_stream = [x, kmeta, vmeta, kwin, vwin, conva, ffn]
    consts = [bias_s, *weights]
    in_specs = ([pl.BlockSpec(memory_space=pltpu.SMEM)]
                + [pl.BlockSpec((SB,) + a.shape[1:], lambda i: (i, 0, 0)) for a in per_stream]
                + [_const_spec(a) for a in consts])
    out_shape = [jax.ShapeDtypeStruct(a.shape, F32) for a in (x, kwin, vwin, conva, ffn)]
    out_specs = [pl.BlockSpec((SB,) + s.shape[1:], lambda i: (i, 0, 0)) for s in out_shape]
    scratch = [pltpu.VMEM((rows, D_MODEL), BF16), pltpu.VMEM((rows, D_IN), F32),
               pltpu.VMEM((HALO + l, D_CONV), F32),
               pltpu.VMEM((rows, D_MODEL), BF16), pltpu.VMEM((rows, D_MODEL), F32),
               pltpu.VMEM((2, SB, 8 + l, FFN_CW), F32), pltpu.VMEM((2, rows, FFN_CW), F32),
               pltpu.VMEM((rows, D_FF), BF16)]
    return pl.pallas_call(
        _sample_kernel,
        grid=(nstream // SB,),
        out_shape=out_shape,
        in_specs=in_specs,
        out_specs=out_specs,
        scratch_shapes=scratch,
        compiler_params=pltpu.CompilerParams(dimension_semantics=("arbitrary",), vmem_limit_bytes=VMEM_LIMIT),
        name="sample_step",
    )(sinks, *per_stream, *consts)


def kernel(x_prompt, x_sample, cache_k_meta, cache_v_meta, cache_k_win, cache_v_win, state_conv_a, state_conv_ffn,
           meta_tokens, rel_bias_table, norm_mix, w_in, conv_dw_w, conv_dw_b, conv_ln_g, conv_ln_b, attn_sinks,
           w_out, norm_ffn, w_up, ffn_dw_w, ffn_dw_b, w_down, norm_final):
    depth = w_in.shape[0]
    assert depth == 1, "single-layer step"
    bsz = x_prompt.shape[0]
    nstream, dec_seq, _ = x_sample.shape
    win = cache_k_win.shape[2]
    assert win == WINDOW

    bp, bs, bm = _bucket_tables(PAST_LEN, dec_seq)
    bias_p, bias_s, bias_m = _build_bias(rel_bias_table, jnp.asarray(bp), jnp.asarray(bs), jnp.asarray(bm))

    row = lambda a: a.reshape(1, -1)
    w_in_b = w_in[0].astype(BF16)
    w_out_b = w_out[0].astype(BF16)
    w_up_b = w_up[0].astype(BF16)
    w_down_b = w_down[0].astype(BF16)
    sinks = attn_sinks[0]
    gmix, gffn, gfin = row(norm_mix[0]), row(norm_ffn[0]), row(norm_final)
    cw = jnp.broadcast_to(conv_dw_w[0][::-1][:, None, :], (CONV_WIDTH, 8, D_CONV))
    dw_b = row(conv_dw_b[0])
    ln_g, ln_b = row(conv_ln_g[0]), row(conv_ln_b[0])
    fw = jnp.broadcast_to(ffn_dw_w[0][:, None, :], (FFN_CONV_WIDTH, 8, D_FF))
    fb = jnp.broadcast_to(ffn_dw_b[0][None, :], (8, D_FF))

    glu_m, k_m, v_m, gate_m = _run_meta(sinks, meta_tokens, bias_m, gmix, w_in_b, cw, dw_b, ln_g, ln_b,
                                        w_out_b, gffn, w_up_b)

    weights = [gmix, w_in_b, cw, dw_b, ln_g, ln_b, w_out_b, gffn, w_up_b, fw, fb, w_down_b, gfin]
    y_p, kwin_p, vwin_p, conva_p, ffn_p = _run_prompt(sinks, x_prompt, glu_m, k_m, v_m, gate_m, bias_p, weights)

    kv = lambda a: a.reshape(a.shape[0], a.shape[1], D_KV)
    y_s, kwin_s, vwin_s, conva_s, ffn_s = _run_sample(
        sinks, x_sample, kv(cache_k_meta[0]), kv(cache_v_meta[0]), kv(cache_k_win[0]), kv(cache_v_win[0]),
        state_conv_a[0], state_conv_ffn[0], bias_s, weights)

    heads = lambda a: a.reshape(1, a.shape[0], a.shape[1], N_KV_HEADS, HEAD_DIM)
    meta_b = lambda a: jnp.broadcast_to(a.reshape(1, 1, N_META, N_KV_HEADS, HEAD_DIM),
                                        (1, bsz, N_META, N_KV_HEADS, HEAD_DIM))
    return (y_p, y_s, meta_b(k_m), meta_b(v_m), heads(kwin_p), heads(vwin_p), conva_p[None], ffn_p[None],
            heads(kwin_s), heads(vwin_s), conva_s[None], ffn_s[None])
```

```python
import functools
import math

import numpy as np
import jax
import jax.numpy as jnp
from jax import lax
from jax.experimental import pallas as pl
from jax.experimental.pallas import tpu as pltpu

F32 = jnp.float32
BF16 = jnp.bfloat16

D_MODEL = 1024
CHUNK = 64
N_META = 16
D_CONV = 512
CONV_WIDTH = 31
N_HEADS = 8
N_KV_HEADS = 2
HEAD_DIM = 64
D_ATTN = N_HEADS * HEAD_DIM
D_KV = N_KV_HEADS * HEAD_DIM
WINDOW = 128
PAST_LEN = 4096
N_BUCKETS = 32
MAX_DISTANCE = 256
D_FF = 2816
FFN_CONV_WIDTH = 3
D_IN = 2 * D_CONV + D_ATTN + 2 * D_KV
EPS = 1e-6
NEG_INF = -1e30
SCALE = HEAD_DIM ** -0.5

_Q0 = 2 * D_CONV
_K0 = _Q0 + D_ATTN
_V0 = _K0 + D_KV

NKEY = 256
BAND = WINDOW + CHUNK
HALO = 32
FFN_CW = 256
TM = 512
SB = 16
VMEM_LIMIT = 60 * 1024 * 1024


def _np_bucket(rel):
    nb = N_BUCKETS // 2
    max_exact = nb // 2
    ret = np.where(rel > 0, nb, 0)
    n = np.abs(rel)
    nf = np.maximum(n, 1).astype(np.float32)
    large = max_exact + (np.log(nf / np.float32(max_exact)) / np.float32(math.log(MAX_DISTANCE / max_exact))
                         * np.float32(nb - max_exact)).astype(np.int32)
    large = np.minimum(large, nb - 1)
    return (ret + np.where(n < max_exact, n, large)).astype(np.int32)


def _bucket_tables(past_len, dec_seq):
    i = np.arange(CHUNK)[:, None]
    j = np.arange(BAND)[None, :]
    m = np.arange(N_META)[None, :]
    band = _np_bucket(j - WINDOW - i)
    pad = np.full((CHUNK, NKEY - BAND - N_META), -1, np.int32)

    def meta_for(c):
        return _np_bucket(m - (N_META + CHUNK * c + i))

    prompt = []
    for c in range(4):
        b = np.where(c - 2 + j // CHUNK >= 0, band, -1)
        prompt.append(np.concatenate([b, meta_for(c), pad], axis=1))
    assert np.array_equal(meta_for(3), meta_for(4096))
    prompt = np.stack(prompt).astype(np.int32)

    i_s = np.arange(dec_seq)[:, None]
    band_s = np.where(j < WINDOW + dec_seq, _np_bucket(j - WINDOW - i_s), -1)
    meta_s = _np_bucket(m - (N_META + past_len + i_s))
    sample = np.concatenate([band_s, meta_s, pad[:dec_seq]], axis=1)[None].astype(np.int32)

    i_m = np.arange(N_META)[:, None]
    meta_self = np.concatenate([np.full((N_META, BAND), -1, np.int32), _np_bucket(m - i_m), pad[:N_META]],
                               axis=1)[None].astype(np.int32)
    return prompt, sample, meta_self


def _bias_kernel(table_ref, bp_ref, bs_ref, bm_ref, op_ref, os_ref, om_ref):
    for b_ref, o_ref in ((bp_ref, op_ref), (bs_ref, os_ref), (bm_ref, om_ref)):
        nv, r, _ = b_ref.shape
        for v in range(nv):
            bk = b_ref[v]
            for g in range(2):
                for pp in range(2):
                    for hf in range(2):
                        h = 4 * g + 2 * pp + hf
                        acc = jnp.full((r, NKEY), NEG_INF, F32)
                        for b in range(N_BUCKETS):
                            acc = jnp.where(bk == b, table_ref[b, h], acc)
                        o_ref[v, g, r * pp:r * (pp + 1), NKEY * hf:NKEY * (hf + 1)] = acc


def _build_bias(table, bp, bs, bm):
    shapes = [jax.ShapeDtypeStruct((b.shape[0], 2, 2 * b.shape[1], 2 * NKEY), F32) for b in (bp, bs, bm)]
    vm = pl.BlockSpec(memory_space=pltpu.VMEM)
    return pl.pallas_call(
        _bias_kernel,
        out_shape=shapes,
        in_specs=[pl.BlockSpec(memory_space=pltpu.SMEM), vm, vm, vm],
        out_specs=[vm, vm, vm],
        name="rel_bias",
    )(table, bp, bs, bm)


def _dot(a, b):
    return jnp.dot(a, b, preferred_element_type=F32)


def _dot_nt(a, b):
    return lax.dot_general(a, b, (((1,), (1,)), ((), ())), preferred_element_type=F32)


def _rms(x, g):
    ms = jnp.mean(x * x, axis=-1, keepdims=True)
    return (x * lax.rsqrt(ms + EPS)) * g


def _lohi(x):
    lo = lax.broadcasted_iota(jnp.int32, x.shape, 1) < HEAD_DIM
    xr = pltpu.roll(x, HEAD_DIM, axis=1)
    zero = jnp.zeros_like(x)
    return (jnp.where(lo, x, zero).astype(BF16), jnp.where(lo, zero, xr).astype(BF16),
            jnp.where(lo, xr, zero).astype(BF16), jnp.where(lo, zero, x).astype(BF16))


def _attention_items(nblk, r, get_q, get_k, get_v, get_bias, sinks_ref, put_out):
    def block(b):
        q = get_q(b) * SCALE
        k_parts = get_k(b)
        v_parts = get_v(b)
        for g in range(N_KV_HEADS):
            qg = jnp.concatenate([q[:, 256 * g:256 * g + 128], q[:, 256 * g + 128:256 * g + 256]],
                                 axis=0).astype(BF16)
            kk = jnp.concatenate([k_parts[2 * g], k_parts[2 * g + 1]], axis=0)
            vv = jnp.concatenate([v_parts[2 * g], v_parts[2 * g + 1]], axis=0)
            s = _dot_nt(qg, kk) + get_bias(b, g)
            rows = []
            for pp in range(2):
                segs = []
                for hf in range(2):
                    sink = sinks_ref[4 * g + 2 * pp + hf]
                    seg = s[r * pp:r * (pp + 1), NKEY * hf:NKEY * (hf + 1)]
                    mx = jnp.maximum(jnp.max(seg, axis=-1, keepdims=True), sink)
                    e = jnp.exp(seg - mx)
                    den = jnp.sum(e, axis=-1, keepdims=True) + jnp.exp(sink - mx)
                    segs.append((e * (1.0 / den)).astype(BF16))
                rows.append(jnp.concatenate(segs, axis=1))
            put_out(b, g, _dot(jnp.concatenate(rows, axis=0), vv))

    return [functools.partial(block, b) for b in range(nblk)]


def _put_heads(mix_ref, row0, r):
    def put(b, g, o):
        base = D_CONV + 256 * g
        mix_ref[row0(b):row0(b) + r, base:base + 128] = o[:r].astype(BF16)
        mix_ref[row0(b):row0(b) + r, base + 128:base + 256] = o[r:].astype(BF16)
    return put


def _in_proj(x_ref, gmix_ref, w_in_ref, hb_ref, z_ref, ns, l, rc):
    g = gmix_ref[...]
    for s in range(ns):
        for r0 in range(0, l, rc):
            hb_ref[s * l + r0:s * l + r0 + rc, :] = _rms(x_ref[s, r0:r0 + rc, :], g).astype(BF16)
    z_ref[:, 0:_Q0] = _dot(hb_ref[...], w_in_ref[:, 0:_Q0])
    z_ref[:, _Q0:D_IN] = _dot(hb_ref[...], w_in_ref[:, _Q0:D_IN])


def _delay(x):
    return pltpu.roll(x, 1, axis=0)


def _conv31_block(glu_buf, u0, rc, cw_ref, l0, n):
    nb = (rc + 8) // 8
    acc = None
    for r in range(7, -1, -1):
        s = None
        for a in range(4):
            d = 8 * a + r
            if d >= CONV_WIDTH:
                continue
            blk = glu_buf[u0 - 8 - 8 * a:u0 + rc - 8 * a, l0:l0 + n].reshape(nb, 8, n)
            term = blk * cw_ref[d, :, l0:l0 + n][None]
            s = term if s is None else s + term
        s = s.reshape(rc + 8, n)
        acc = s if acc is None else s + _delay(acc)
    return acc[8:]


def _run(items):
    for item in items:
        item()


def _interleave(a, b):
    out, ia, ib = [], 0, 0
    while ia < len(a) or ib < len(b):
        if ib >= len(b) or (ia < len(a) and ia * len(b) <= ib * len(a)):
            out.append(a[ia])
            ia += 1
        else:
            out.append(b[ib])
            ib += 1
    return out


def _conv_items(z_ref, row0, l, glu_buf, cw_ref, dw_b_ref, ln_g_ref, ln_b_ref, mix_ref, rc):
    rn = min(rc, 64)

    def glu(r0):
        a = z_ref[row0 + r0:row0 + r0 + rc, 0:D_CONV]
        b = z_ref[row0 + r0:row0 + r0 + rc, D_CONV:2 * D_CONV]
        glu_buf[HALO + r0:HALO + r0 + rc, :] = a * jax.nn.sigmoid(b)

    def conv(r0):
        for l0 in range(0, D_CONV, 128):
            z_ref[row0 + r0:row0 + r0 + rc, l0:l0 + 128] = _conv31_block(glu_buf, HALO + r0, rc, cw_ref, l0, 128)

    def norm(r0):
        for q0 in range(r0, r0 + rc, rn):
            acc = z_ref[row0 + q0:row0 + q0 + rn, 0:D_CONV] + dw_b_ref[...]
            mu = jnp.mean(acc, axis=-1, keepdims=True)
            d = acc - mu
            var = jnp.mean(d * d, axis=-1, keepdims=True)
            yn = (d * lax.rsqrt(var + EPS)) * ln_g_ref[...] + ln_b_ref[...]
            mix_ref[row0 + q0:row0 + q0 + rn, 0:D_CONV] = (yn * jax.nn.sigmoid(yn)).astype(BF16)

    items = []
    for r0 in range(0, l, rc):
        items += [functools.partial(glu, r0), functools.partial(conv, r0), functools.partial(norm, r0)]
    return items


def _seq_pieces(m0, mc, l):
    pc = min(l, mc)
    return [(p0,) + divmod(m0 + p0, l) + (pc,) for p0 in range(0, mc, pc)]


def _out_proj(x_ref, mix_ref, w_out_ref, gffn_ref, x1_ref, hb_ref, ns, l, mc):
    g = gffn_ref[...]
    for m0 in range(0, ns * l, mc):
        acc = _dot(mix_ref[m0:m0 + mc, :], w_out_ref[...])
        for p0, s, r0, pc in _seq_pieces(m0, mc, l):
            x1 = x_ref[s, r0:r0 + pc, :] + acc[p0:p0 + pc]
            x1_ref[m0 + p0:m0 + p0 + pc, :] = x1
            hb_ref[m0 + p0:m0 + p0 + pc, :] = _rms(x1, g).astype(BF16)


def _ffn_hidden(hb_ref, w_up_ref, fw_ref, fb_ref, gbuf, ubuf, y_ref, put_halo, save_carry, ns, l, rc):
    nch = D_FF // FFN_CW
    nb = (rc + 8) // 8

    def matmuls(ci):
        c = ci * FFN_CW
        gb = gbuf.at[ci % 2]
        ug = _dot(hb_ref[...], w_up_ref[:, c:c + FFN_CW])
        for s in range(ns):
            put_halo(gb, s, slice(c, c + FFN_CW))
            gb[s, 8:8 + l, :] = ug[s * l:(s + 1) * l]
        ubuf[ci % 2] = _dot(hb_ref[...], w_up_ref[:, D_FF + c:D_FF + c + FFN_CW])

    def elementwise(ci):
        c = ci * FFN_CW
        gb = gbuf.at[ci % 2]
        ub = ubuf.at[ci % 2]
        cols = slice(c, c + FFN_CW)
        w0 = fw_ref[0, :, cols][None]
        w1 = fw_ref[1, :, cols][None]
        w2 = fw_ref[2, :, cols][None]
        bb = fb_ref[:, cols][None]
        for s in range(ns):
            for r0 in range(0, l, rc):
                g = gb[s, r0:r0 + rc + 8, :].reshape(nb, 8, FFN_CW)
                t = (g * w0).reshape(rc + 8, FFN_CW)
                t = (g * w1).reshape(rc + 8, FFN_CW) + _delay(t)
                t = (g * w2 + bb).reshape(rc + 8, FFN_CW) + _delay(t)
                cv = t[8:]
                y = (cv * jax.nn.sigmoid(cv)) * ub[s * l + r0:s * l + r0 + rc, :]
                y_ref[s * l + r0:s * l + r0 + rc, cols] = y.astype(BF16)
            save_carry(gb, s, cols)

    matmuls(0)
    for ci in range(nch):
        if ci + 1 < nch:
            matmuls(ci + 1)
        elementwise(ci)


def _ffn_out(x1_ref, y_ref, w_down_ref, gfin_ref, o_ref, ns, l, mc):
    g = gfin_ref[...]
    for m0 in range(0, ns * l, mc):
        acc = _dot(y_ref[m0:m0 + mc, :], w_down_ref[...])
        for p0, s, r0, pc in _seq_pieces(m0, mc, l):
            x2 = x1_ref[m0 + p0:m0 + p0 + pc, :] + acc[p0:p0 + pc]
            o_ref[s, r0:r0 + pc, :] = _rms(x2, g)


def _meta_kernel(sinks_ref, x_ref, bias_ref, gmix_ref, w_in_ref, cw_ref, dw_b_ref, ln_g_ref, ln_b_ref,
                 w_out_ref, gffn_ref, w_upg_ref,
                 glu_o, k_o, v_o, gate_o,
                 hb_ref, z_ref, glu_buf, mix_ref):
    l = N_META
    _in_proj(x_ref, gmix_ref, w_in_ref, hb_ref, z_ref, 1, l, l)
    glu_buf[0:HALO, :] = jnp.zeros((HALO, D_CONV), F32)
    _run(_conv_items(z_ref, 0, l, glu_buf, cw_ref, dw_b_ref, ln_g_ref, ln_b_ref, mix_ref, l))
    glu_o[...] = glu_buf[HALO:HALO + l, :]
    k = z_ref[:, _K0:_K0 + D_KV]
    v = z_ref[:, _V0:_V0 + D_KV]
    k_o[...] = k
    v_o[...] = v
    zb = jnp.zeros((BAND, D_KV), BF16)
    zp = jnp.zeros((NKEY - BAND - N_META, D_KV), BF16)
    k_parts = [jnp.concatenate([zb, p, zp], axis=0) for p in _lohi(k)]
    v_parts = [jnp.concatenate([zb, p, zp], axis=0) for p in _lohi(v)]
    _run(_attention_items(1, l, lambda b: z_ref[:, _Q0:_Q0 + D_ATTN], lambda b: k_parts, lambda b: v_parts,
                          lambda b, g: bias_ref[0, g], sinks_ref, _put_heads(mix_ref, lambda b: 0, l)))
    x1 = x_ref[0] + _dot(mix_ref[...], w_out_ref[...])
    h2 = _rms(x1, gffn_ref[...]).astype(BF16)
    gate = _dot(h2, w_upg_ref[...])
    gate_o[...] = gate[N_META - 8:N_META, :]


def _run_meta(sinks, meta_tokens, bias_m, gmix, w_in, cw, dw_b, ln_g, ln_b, w_out, gffn, w_up):
    l = N_META
    full = lambda a: pl.BlockSpec(a.shape, lambda i: (0,) * a.ndim)
    x = meta_tokens[None]
    ins = [x, bias_m, gmix, w_in, cw, dw_b, ln_g, ln_b, w_out, gffn]
    in_specs = [pl.BlockSpec(memory_space=pltpu.SMEM)] + [full(a) for a in ins]
    in_specs.append(pl.BlockSpec((D_MODEL, D_FF), lambda i: (0, 0)))
    out_shape = [jax.ShapeDtypeStruct((l, D_CONV), F32), jax.ShapeDtypeStruct((l, D_KV), F32),
                 jax.ShapeDtypeStruct((l, D_KV), F32), jax.ShapeDtypeStruct((8, D_FF), F32)]
    return pl.pallas_call(
        _meta_kernel,
        grid=(1,),
        out_shape=out_shape,
        in_specs=in_specs,
        out_specs=[pl.BlockSpec(s.shape, lambda i: (0, 0)) for s in out_shape],
        scratch_shapes=[pltpu.VMEM((l, D_MODEL), BF16), pltpu.VMEM((l, D_IN), F32),
                        pltpu.VMEM((HALO + l, D_CONV), F32), pltpu.VMEM((l, D_MODEL), BF16)],
        compiler_params=pltpu.CompilerParams(vmem_limit_bytes=VMEM_LIMIT),
        name="meta_tokens",
    )(sinks, *ins, w_up)


def _prompt_kernel(sinks_ref, x_ref, glu_m_ref, k_m_ref, v_m_ref, gate_m_ref, bias_ref,
                   gmix_ref, w_in_ref, cw_ref, dw_b_ref, ln_g_ref, ln_b_ref, w_out_ref, gffn_ref,
                   w_up_ref, fw_ref, fb_ref, w_down_ref, gfin_ref,
                   y_o, kwin_o, vwin_o, conva_o, ffn_o,
                   hb_ref, z_ref, glu_buf, kv_buf, mix_ref, x1_ref, gbuf, ubuf, yh_ref, gcarry):
    t = pl.program_id(1)
    l = TM
    rc = 128

    @pl.when(t == 0)
    def _():
        glu_buf[0:HALO - N_META, :] = jnp.zeros((HALO - N_META, D_CONV), F32)
        glu_buf[HALO - N_META:HALO, :] = glu_m_ref[...]
        kwin_o[0] = jnp.zeros((WINDOW, D_KV), F32)
        vwin_o[0] = jnp.zeros((WINDOW, D_KV), F32)
        gcarry[...] = gate_m_ref[...]

    _in_proj(x_ref, gmix_ref, w_in_ref, hb_ref, z_ref, 1, l, rc)

    conv_items = _conv_items(z_ref, 0, l, glu_buf, cw_ref, dw_b_ref, ln_g_ref, ln_b_ref, mix_ref, rc)

    def kv_layouts():
        for idx, (win_o, c0) in enumerate(((kwin_o, _K0), (vwin_o, _V0))):
            for i, p in enumerate(_lohi(win_o[0])):
                kv_buf[4 * idx + i, 0:WINDOW, :] = p
            for r0 in range(0, l, rc):
                for i, p in enumerate(_lohi(z_ref[r0:r0 + rc, c0:c0 + D_KV])):
                    kv_buf[4 * idx + i, WINDOW + r0:WINDOW + r0 + rc, :] = p
            win_o[0] = z_ref[l - WINDOW:l, c0:c0 + D_KV]

    zp = jnp.zeros((NKEY - BAND - N_META, D_KV), BF16)
    km = [jnp.concatenate([p, zp], axis=0) for p in _lohi(k_m_ref[...])]
    vm = [jnp.concatenate([p, zp], axis=0) for p in _lohi(v_m_ref[...])]

    def variant(c):
        return jnp.where(t == 0, c, 3) if c < 3 else 3

    rows = lambda c: c * CHUNK
    attn_items = [kv_layouts] + _attention_items(
        l // CHUNK, CHUNK,
        lambda c: z_ref[rows(c):rows(c) + CHUNK, _Q0:_Q0 + D_ATTN],
        lambda c: [jnp.concatenate([kv_buf[i, rows(c):rows(c) + BAND, :], km[i]], axis=0) for i in range(4)],
        lambda c: [jnp.concatenate([kv_buf[4 + i, rows(c):rows(c) + BAND, :], vm[i]], axis=0) for i in range(4)],
        lambda c, g: bias_ref[variant(c), g],
        sinks_ref, _put_heads(mix_ref, rows, CHUNK))

    _run(_interleave(conv_items, attn_items))
    conva_o[0] = glu_buf[HALO + l - (CONV_WIDTH - 1):HALO + l, :]
    glu_buf[0:HALO, :] = glu_buf[l:l + HALO, :]

    _out_proj(x_ref, mix_ref, w_out_ref, gffn_ref, x1_ref, hb_ref, 1, l, 256)

    def put_halo(gb, s, cols):
        gb[s, 0:8, :] = gcarry[:, cols]

    def save_carry(gb, s, cols):
        gcarry[:, cols] = gb[s, l:l + 8, :]

    _ffn_hidden(hb_ref, w_up_ref, fw_ref, fb_ref, gbuf, ubuf, yh_ref, put_halo, save_carry, 1, l, rc)
    ffn_o[0] = gcarry[8 - (FFN_CONV_WIDTH - 1):8, :]
    _ffn_out(x1_ref, yh_ref, w_down_ref, gfin_ref, y_o, 1, l, 256)


def _const_spec(a):
    return pl.BlockSpec(a.shape, lambda *_: (0,) * a.ndim, pipeline_mode=pl.Buffered(1))


def _run_prompt(sinks, x, glu_m, k_m, v_m, gate_m, bias_p, weights):
    bsz, seq, _ = x.shape
    assert seq % TM == 0 and TM % CHUNK == 0 and TM >= WINDOW
    consts = [glu_m, k_m, v_m, gate_m, bias_p, *weights]
    in_specs = ([pl.BlockSpec(memory_space=pltpu.SMEM), pl.BlockSpec((1, TM, D_MODEL), lambda b, t: (b, t, 0))]
                + [_const_spec(a) for a in consts])
    out_shape = [jax.ShapeDtypeStruct((bsz, seq, D_MODEL), F32),
                 jax.ShapeDtypeStruct((bsz, WINDOW, D_KV), F32), jax.ShapeDtypeStruct((bsz, WINDOW, D_KV), F32),
                 jax.ShapeDtypeStruct((bsz, CONV_WIDTH - 1, D_CONV), F32),
                 jax.ShapeDtypeStruct((bsz, FFN_CONV_WIDTH - 1, D_FF), F32)]
    out_specs = [pl.BlockSpec((1, TM, D_MODEL), lambda b, t: (b, t, 0))] + [
        pl.BlockSpec((1,) + s.shape[1:], lambda b, t: (b, 0, 0)) for s in out_shape[1:]]
    scratch = [pltpu.VMEM((TM, D_MODEL), BF16), pltpu.VMEM((TM, D_IN), F32),
               pltpu.VMEM((HALO + TM, D_CONV), F32), pltpu.VMEM((8, WINDOW + TM, D_KV), BF16),
               pltpu.VMEM((TM, D_MODEL), BF16), pltpu.VMEM((TM, D_MODEL), F32),
               pltpu.VMEM((2, 1, 8 + TM, FFN_CW), F32), pltpu.VMEM((2, TM, FFN_CW), F32),
               pltpu.VMEM((TM, D_FF), BF16), pltpu.VMEM((8, D_FF), F32)]
    return pl.pallas_call(
        _prompt_kernel,
        grid=(bsz, seq // TM),
        out_shape=out_shape,
        in_specs=in_specs,
        out_specs=out_specs,
        scratch_shapes=scratch,
        compiler_params=pltpu.CompilerParams(dimension_semantics=("arbitrary", "arbitrary"),
                                             vmem_limit_bytes=VMEM_LIMIT),
        name="prompt_step",
    )(sinks, x, *consts)


def _sample_kernel(sinks_ref, x_ref, kmeta_ref, vmeta_ref, kwin_ref, vwin_ref, conva_ref, ffn_ref, bias_ref,
                   gmix_ref, w_in_ref, cw_ref, dw_b_ref, ln_g_ref, ln_b_ref, w_out_ref, gffn_ref,
                   w_up_ref, fw_ref, fb_ref, w_down_ref, gfin_ref,
                   y_o, kwin_o, vwin_o, conva_o, ffn_o,
                   hb_ref, z_ref, glu_buf, mix_ref, x1_ref, gbuf, ubuf, yh_ref):
    ns, l, _ = x_ref.shape
    _in_proj(x_ref, gmix_ref, w_in_ref, hb_ref, z_ref, ns, l, l)
    glu_buf[0:HALO - 30, :] = jnp.zeros((HALO - 30, D_CONV), F32)
    zgap = jnp.zeros((BAND - WINDOW - l, D_KV), BF16)
    zp = jnp.zeros((NKEY - BAND - N_META, D_KV), BF16)
    def load_state(s):
        glu_buf[HALO - (CONV_WIDTH - 1):HALO, :] = conva_ref[s]

    def save_state(s):
        conva_o[s] = glu_buf[HALO + l - (CONV_WIDTH - 1):HALO + l, :]
        for win_ref, win_o, c0 in ((kwin_ref, kwin_o, _K0), (vwin_ref, vwin_o, _V0)):
            win_o[s, 0:WINDOW - l, :] = win_ref[s, l:WINDOW, :]
            win_o[s, WINDOW - l:WINDOW, :] = z_ref[s * l:(s + 1) * l, c0:c0 + D_KV]

    conv_items = []
    for s in range(ns):
        conv_items += ([functools.partial(load_state, s)]
                       + _conv_items(z_ref, s * l, l, glu_buf, cw_ref, dw_b_ref, ln_g_ref, ln_b_ref, mix_ref, l)
                       + [functools.partial(save_state, s)])

    def key_slots(win_ref, meta_ref, c0):
        def get(s):
            new = z_ref[s * l:(s + 1) * l, c0:c0 + D_KV]
            return [jnp.concatenate([pw, pn, zgap, pm, zp], axis=0)
                    for pw, pn, pm in zip(_lohi(win_ref[s]), _lohi(new), _lohi(meta_ref[s]))]
        return get

    rows = lambda s: s * l
    attn_items = _attention_items(ns, l, lambda s: z_ref[rows(s):rows(s) + l, _Q0:_Q0 + D_ATTN],
                                  key_slots(kwin_ref, kmeta_ref, _K0), key_slots(vwin_ref, vmeta_ref, _V0),
                                  lambda s, g: bias_ref[0, g], sinks_ref, _put_heads(mix_ref, rows, l))
    _run(_interleave(conv_items, attn_items))

    _out_proj(x_ref, mix_ref, w_out_ref, gffn_ref, x1_ref, hb_ref, ns, l, 256)

    nprev = FFN_CONV_WIDTH - 1
    for i in range(2):
        for s in range(ns):
            gbuf[i, s, 0:8, :] = jnp.zeros((8, FFN_CW), F32)

    def put_halo(gb, s, cols):
        gb[s, 8 - nprev:8, :] = ffn_ref[s, :, cols]

    def save_carry(gb, s, cols):
        ffn_o[s, :, cols] = gb[s, 8 + l - nprev:8 + l, :]

    _ffn_hidden(hb_ref, w_up_ref, fw_ref, fb_ref, gbuf, ubuf, yh_ref, put_halo, save_carry, ns, l, l)
    _ffn_out(x1_ref, yh_ref, w_down_ref, gfin_ref, y_o, ns, l, 256)


def _run_sample(sinks, x, kmeta, vmeta, kwin, vwin, conva, ffn, bias_s, weights):
    nstream, l, _ = x.shape
    assert nstream % SB == 0 and l % 16 == 0 and l <= CHUNK and kwin.shape[1] == WINDOW
    rows = SB * l
    per_stream = [x, kmeta, vmeta, kwin, vwin, conva, ffn]
    consts = [bias_s, *weights]
    in_specs = ([pl.BlockSpec(memory_space=pltpu.SMEM)]
                + [pl.BlockSpec((SB,) + a.shape[1:], lambda i: (i, 0, 0)) for a in per_stream]
                + [_const_spec(a) for a in consts])
    out_shape = [jax.ShapeDtypeStruct(a.shape, F32) for a in (x, kwin, vwin, conva, ffn)]
    out_specs = [pl.BlockSpec((SB,) + s.shape[1:], lambda i: (i, 0, 0)) for s in out_shape]
    scratch = [pltpu.VMEM((rows, D_MODEL), BF16), pltpu.VMEM((rows, D_IN), F32),
               pltpu.VMEM((HALO + l, D_CONV), F32),
               pltpu.VMEM((rows, D_MODEL), BF16), pltpu.VMEM((rows, D_MODEL), F32),
               pltpu.VMEM((2, SB, 8 + l, FFN_CW), F32), pltpu.VMEM((2, rows, FFN_CW), F32),
               pltpu.VMEM((rows, D_FF), BF16)]
    return pl.pallas_call(
        _sample_kernel,
        grid=(nstream // SB,),
        out_shape=out_shape,
        in_specs=in_specs,
        out_specs=out_specs,
        scratch_shapes=scratch,
        compiler_params=pltpu.CompilerParams(dimension_semantics=("arbitrary",), vmem_limit_bytes=VMEM_LIMIT),
        name="sample_step",
    )(sinks, *per_stream, *consts)


def kernel(x_prompt, x_sample, cache_k_meta, cache_v_meta, cache_k_win, cache_v_win, state_conv_a, state_conv_ffn,
           meta_tokens, rel_bias_table, norm_mix, w_in, conv_dw_w, conv_dw_b, conv_ln_g, conv_ln_b, attn_sinks,
           w_out, norm_ffn, w_up, ffn_dw_w, ffn_dw_b, w_down, norm_final):
    depth = w_in.shape[0]
    assert depth == 1, "single-layer step"
    bsz = x_prompt.shape[0]
    nstream, dec_seq, _ = x_sample.shape
    win = cache_k_win.shape[2]
    assert win == WINDOW

    bp, bs, bm = _bucket_tables(PAST_LEN, dec_seq)
    bias_p, bias_s, bias_m = _build_bias(rel_bias_table, jnp.asarray(bp), jnp.asarray(bs), jnp.asarray(bm))

    row = lambda a: a.reshape(1, -1)
    w_in_b = w_in[0].astype(BF16)
    w_out_b = w_out[0].astype(BF16)
    w_up_b = w_up[0].astype(BF16)
    w_down_b = w_down[0].astype(BF16)
    sinks = attn_sinks[0]
    gmix, gffn, gfin = row(norm_mix[0]), row(norm_ffn[0]), row(norm_final)
    cw = jnp.broadcast_to(conv_dw_w[0][::-1][:, None, :], (CONV_WIDTH, 8, D_CONV))
    dw_b = row(conv_dw_b[0])
    ln_g, ln_b = row(conv_ln_g[0]), row(conv_ln_b[0])
    fw = jnp.broadcast_to(ffn_dw_w[0][:, None, :], (FFN_CONV_WIDTH, 8, D_FF))
    fb = jnp.broadcast_to(ffn_dw_b[0][None, :], (8, D_FF))

    glu_m, k_m, v_m, gate_m = _run_meta(sinks, meta_tokens, bias_m, gmix, w_in_b, cw, dw_b, ln_g, ln_b,
                                        w_out_b, gffn, w_up_b)

    weights = [gmix, w_in_b, cw, dw_b, ln_g, ln_b, w_out_b, gffn, w_up_b, fw, fb, w_down_b, gfin]
    y_p, kwin_p, vwin_p, conva_p, ffn_p = _run_prompt(sinks, x_prompt, glu_m, k_m, v_m, gate_m, bias_p, weights)

    kv = lambda a: a.reshape(a.shape[0], a.shape[1], D_KV)
    y_s, kwin_s, vwin_s, conva_s, ffn_s = _run_sample(
        sinks, x_sample, kv(cache_k_meta[0]), kv(cache_v_meta[0]), kv(cache_k_win[0]), kv(cache_v_win[0]),
        state_conv_a[0], state_conv_ffn[0], bias_s, weights)

    heads = lambda a: a.reshape(1, a.shape[0], a.shape[1], N_KV_HEADS, HEAD_DIM)
    meta_b = lambda a: jnp.broadcast_to(a.reshape(1, 1, N_META, N_KV_HEADS, HEAD_DIM),
                                        (1, bsz, N_META, N_KV_HEADS, HEAD_DIM))
    return (y_p, y_s, meta_b(k_m), meta_b(v_m), heads(kwin_p), heads(vwin_p), conva_p[None], ffn_p[None],
            heads(kwin_s), heads(vwin_s), conva_s[None], ffn_s[None])
```

```python
import functools
import math

import numpy as np
import jax
import jax.numpy as jnp
from jax import lax
from jax.experimental import pallas as pl
from jax.experimental.pallas import tpu as pltpu

F32 = jnp.float32
BF16 = jnp.bfloat16

D_MODEL = 1024
CHUNK = 64
N_META = 16
D_CONV = 512
CONV_WIDTH = 31
N_HEADS = 8
N_KV_HEADS = 2
HEAD_DIM = 64
HEADS_PER_KV = N_HEADS // N_KV_HEADS
D_ATTN = N_HEADS * HEAD_DIM
D_KV = N_KV_HEADS * HEAD_DIM
WINDOW = 128
PAST_LEN = 4096
N_BUCKETS = 32
MAX_DISTANCE = 256
D_FF = 2816
FFN_CONV_WIDTH = 3
D_IN = 2 * D_CONV + D_ATTN + 2 * D_KV
EPS = 1e-6
NEG_INF = -1e30
SCALE = HEAD_DIM ** -0.5

_Q0 = 2 * D_CONV
_K0 = _Q0 + D_ATTN
_V0 = _K0 + D_KV

NKEY = 256
BAND = WINDOW + CHUNK
HALO = 32
FFN_CW = 256
TM = 512
SB = 16
VMEM_LIMIT = 60 * 1024 * 1024


def _np_bucket(rel):
    nb = N_BUCKETS // 2
    max_exact = nb // 2
    ret = np.where(rel > 0, nb, 0)
    n = np.abs(rel)
    nf = np.maximum(n, 1).astype(np.float32)
    large = max_exact + (np.log(nf / np.float32(max_exact)) / np.float32(math.log(MAX_DISTANCE / max_exact))
                         * np.float32(nb - max_exact)).astype(np.int32)
    large = np.minimum(large, nb - 1)
    return (ret + np.where(n < max_exact, n, large)).astype(np.int32)


def _bucket_tables(past_len, dec_seq):
    i = np.arange(CHUNK)[:, None]
    j = np.arange(BAND)[None, :]
    m = np.arange(N_META)[None, :]
    band = _np_bucket(j - WINDOW - i)
    pad = np.full((CHUNK, NKEY - BAND - N_META), -1, np.int32)

    def meta_for(c):
        return _np_bucket(m - (N_META + CHUNK * c + i))

    prompt = []
    for c in range(4):
        b = np.where(c - 2 + j // CHUNK >= 0, band, -1)
        prompt.append(np.concatenate([b, meta_for(c), pad], axis=1))
    assert np.array_equal(meta_for(3), meta_for(4096))
    prompt = np.stack(prompt).astype(np.int32)

    i_s = np.arange(dec_seq)[:, None]
    band_s = np.where(j < WINDOW + dec_seq, _np_bucket(j - WINDOW - i_s), -1)
    meta_s = _np_bucket(m - (N_META + past_len + i_s))
    sample = np.concatenate([band_s, meta_s, pad[:dec_seq]], axis=1)[None].astype(np.int32)

    i_m = np.arange(N_META)[:, None]
    meta_self = np.concatenate([np.full((N_META, BAND), -1, np.int32), _np_bucket(m - i_m), pad[:N_META]],
                               axis=1)[None].astype(np.int32)
    return prompt, sample, meta_self


def _bias_kernel(table_ref, bp_ref, bs_ref, bm_ref, op_ref, os_ref, om_ref):
    for b_ref, o_ref in ((bp_ref, op_ref), (bs_ref, os_ref), (bm_ref, om_ref)):
        nv, r, _ = b_ref.shape
        for v in range(nv):
            bk = b_ref[v]
            for g in range(N_KV_HEADS):
                for hl in range(HEADS_PER_KV):
                    acc = jnp.full((r, NKEY), NEG_INF, F32)
                    for b in range(N_BUCKETS):
                        acc = jnp.where(bk == b, table_ref[b, HEADS_PER_KV * g + hl], acc)
                    o_ref[v, g, r * hl:r * (hl + 1), :] = acc


def _build_bias(table, bp, bs, bm):
    shapes = [jax.ShapeDtypeStruct((b.shape[0], N_KV_HEADS, HEADS_PER_KV * b.shape[1], NKEY), F32)
              for b in (bp, bs, bm)]
    vm = pl.BlockSpec(memory_space=pltpu.VMEM)
    return pl.pallas_call(
        _bias_kernel,
        out_shape=shapes,
        in_specs=[pl.BlockSpec(memory_space=pltpu.SMEM), vm, vm, vm],
        out_specs=[vm, vm, vm],
        name="rel_bias",
    )(table, bp, bs, bm)


def _dot(a, b):
    return jnp.dot(a, b, preferred_element_type=F32)


def _dot_nt(a, b):
    return lax.dot_general(a, b, (((1,), (1,)), ((), ())), preferred_element_type=F32)


def _rms(x, g):
    ms = jnp.mean(x * x, axis=-1, keepdims=True)
    return (x * lax.rsqrt(ms + EPS)) * g


def _swap_halves(x):
    return pltpu.roll(x, HEAD_DIM, axis=1)


def _attention_items(nblk, r, get_q, get_k, get_v, get_bias, sinks_ref, put_out):
    def block(b):
        q = get_q(b) * SCALE
        kb = get_k(b)
        vb = get_v(b)
        lo = lax.broadcasted_iota(jnp.int32, (r, 2 * HEAD_DIM), 1) < HEAD_DIM
        for g in range(N_KV_HEADS):
            keep = lo if g == 0 else jnp.logical_not(lo)
            parts = []
            for hl in range(HEADS_PER_KV):
                h = HEADS_PER_KV * g + hl
                qh = q[:, 128 * (h // 2):128 * (h // 2) + 128]
                if h % 2 != g:
                    qh = _swap_halves(qh)
                parts.append(jnp.where(keep, qh, 0.0))
            qs = jnp.concatenate(parts, axis=0).astype(BF16)
            s = _dot_nt(qs, kb) + get_bias(b, g)
            probs = []
            for hl in range(HEADS_PER_KV):
                sink = sinks_ref[HEADS_PER_KV * g + hl]
                seg = s[r * hl:r * (hl + 1), :]
                mx = jnp.maximum(jnp.max(seg, axis=-1, keepdims=True), sink)
                e = jnp.exp(seg - mx)
                den = jnp.sum(e, axis=-1, keepdims=True) + jnp.exp(sink - mx)
                probs.append((e * (1.0 / den)).astype(BF16))
            o = _dot(jnp.concatenate(probs, axis=0), vb)
            for pp in range(2):
                even = o[r * 2 * pp:r * (2 * pp + 1)]
                odd = o[r * (2 * pp + 1):r * (2 * pp + 2)]
                if g == 1:
                    even = _swap_halves(even)
                else:
                    odd = _swap_halves(odd)
                put_out(b, g, pp, jnp.where(lo, even, odd))

    return [functools.partial(block, b) for b in range(nblk)]


def _put_heads(mix_ref, row0, r):
    def put(b, g, pp, o):
        base = D_CONV + 256 * g + 128 * pp
        mix_ref[row0(b):row0(b) + r, base:base + 128] = o.astype(BF16)
    return put


def _in_proj(x_ref, gmix_ref, w_in_ref, hb_ref, z_ref, ns, l, rc):
    g = gmix_ref[...]
    for s in range(ns):
        for r0 in range(0, l, rc):
            hb_ref[s * l + r0:s * l + r0 + rc, :] = _rms(x_ref[s, r0:r0 + rc, :], g).astype(BF16)
    z_ref[:, 0:_Q0] = _dot(hb_ref[...], w_in_ref[:, 0:_Q0])
    z_ref[:, _Q0:D_IN] = _dot(hb_ref[...], w_in_ref[:, _Q0:D_IN])


def _delay(x):
    return pltpu.roll(x, 1, axis=0)


def _conv31_block(glu_buf, u0, rc, cw_ref, l0, n):
    nb = (rc + 8) // 8
    acc = None
    for r in range(7, -1, -1):
        s = None
        for a in range(4):
            d = 8 * a + r
            if d >= CONV_WIDTH:
                continue
            blk = glu_buf[u0 - 8 - 8 * a:u0 + rc - 8 * a, l0:l0 + n].reshape(nb, 8, n)
            term = blk * cw_ref[d, :, l0:l0 + n][None]
            s = term if s is None else s + term
        s = s.reshape(rc + 8, n)
        acc = s if acc is None else s + _delay(acc)
    return acc[8:]


def _run(items):
    for item in items:
        item()


def _interleave(a, b):
    out, ia, ib = [], 0, 0
    while ia < len(a) or ib < len(b):
        if ib >= len(b) or (ia < len(a) and ia * len(b) <= ib * len(a)):
            out.append(a[ia])
            ia += 1
        else:
            out.append(b[ib])
            ib += 1
    return out


def _conv_items(z_ref, row0, l, glu_buf, cw_ref, dw_b_ref, ln_g_ref, ln_b_ref, mix_ref, rc):
    rn = min(rc, 64)

    def glu(r0):
        a = z_ref[row0 + r0:row0 + r0 + rc, 0:D_CONV]
        b = z_ref[row0 + r0:row0 + r0 + rc, D_CONV:2 * D_CONV]
        glu_buf[HALO + r0:HALO + r0 + rc, :] = a * jax.nn.sigmoid(b)

    def conv(r0):
        for l0 in range(0, D_CONV, 128):
            z_ref[row0 + r0:row0 + r0 + rc, l0:l0 + 128] = _conv31_block(glu_buf, HALO + r0, rc, cw_ref, l0, 128)

    def norm(r0):
        for q0 in range(r0, r0 + rc, rn):
            acc = z_ref[row0 + q0:row0 + q0 + rn, 0:D_CONV] + dw_b_ref[...]
            mu = jnp.mean(acc, axis=-1, keepdims=True)
            d = acc - mu
            var = jnp.mean(d * d, axis=-1, keepdims=True)
            yn = (d * lax.rsqrt(var + EPS)) * ln_g_ref[...] + ln_b_ref[...]
            mix_ref[row0 + q0:row0 + q0 + rn, 0:D_CONV] = (yn * jax.nn.sigmoid(yn)).astype(BF16)

    items = []
    for r0 in range(0, l, rc):
        items += [functools.partial(glu, r0), functools.partial(conv, r0), functools.partial(norm, r0)]
    return items


def _seq_pieces(m0, mc, l):
    pc = min(l, mc)
    return [(p0,) + divmod(m0 + p0, l) + (pc,) for p0 in range(0, mc, pc)]


def _out_proj(x_ref, mix_ref, w_out_ref, gffn_ref, x1_ref, hb_ref, ns, l, mc):
    g = gffn_ref[...]
    for m0 in range(0, ns * l, mc):
        acc = _dot(mix_ref[m0:m0 + mc, :], w_out_ref[...])
        for p0, s, r0, pc in _seq_pieces(m0, mc, l):
            x1 = x_ref[s, r0:r0 + pc, :] + acc[p0:p0 + pc]
            x1_ref[m0 + p0:m0 + p0 + pc, :] = x1
            hb_ref[m0 + p0:m0 + p0 + pc, :] = _rms(x1, g).astype(BF16)


def _ffn_hidden(hb_ref, w_up_ref, fw_ref, fb_ref, gbuf, ubuf, y_ref, put_halo, save_carry, ns, l, rc):
    nch = D_FF // FFN_CW
    nb = (rc + 8) // 8

    def matmuls(ci):
        c = ci * FFN_CW
        gb = gbuf.at[ci % 2]
        ug = _dot(hb_ref[...], w_up_ref[:, c:c + FFN_CW])
        for s in range(ns):
            put_halo(gb, s, slice(c, c + FFN_CW))
            gb[s, 8:8 + l, :] = ug[s * l:(s + 1) * l]
        ubuf[ci % 2] = _dot(hb_ref[...], w_up_ref[:, D_FF + c:D_FF + c + FFN_CW])

    def elementwise(ci):
        c = ci * FFN_CW
        gb = gbuf.at[ci % 2]
        ub = ubuf.at[ci % 2]
        cols = slice(c, c + FFN_CW)
        w0 = fw_ref[0, :, cols][None]
        w1 = fw_ref[1, :, cols][None]
        w2 = fw_ref[2, :, cols][None]
        bb = fb_ref[:, cols][None]
        for s in range(ns):
            for r0 in range(0, l, rc):
                g = gb[s, r0:r0 + rc + 8, :].reshape(nb, 8, FFN_CW)
                t = (g * w0).reshape(rc + 8, FFN_CW)
                t = (g * w1).reshape(rc + 8, FFN_CW) + _delay(t)
                t = (g * w2 + bb).reshape(rc + 8, FFN_CW) + _delay(t)
                cv = t[8:]
                y = (cv * jax.nn.sigmoid(cv)) * ub[s * l + r0:s * l + r0 + rc, :]
                y_ref[s * l + r0:s * l + r0 + rc, cols] = y.astype(BF16)
            save_carry(gb, s, cols)

    matmuls(0)
    for ci in range(nch):
        if ci + 1 < nch:
            matmuls(ci + 1)
        elementwise(ci)


def _ffn_out(x1_ref, y_ref, w_down_ref, gfin_ref, o_ref, ns, l, mc):
    g = gfin_ref[...]
    for m0 in range(0, ns * l, mc):
        acc = _dot(y_ref[m0:m0 + mc, :], w_down_ref[...])
        for p0, s, r0, pc in _seq_pieces(m0, mc, l):
            x2 = x1_ref[m0 + p0:m0 + p0 + pc, :] + acc[p0:p0 + pc]
            o_ref[s, r0:r0 + pc, :] = _rms(x2, g)


def _meta_kernel(sinks_ref, x_ref, bias_ref, gmix_ref, w_in_ref, cw_ref, dw_b_ref, ln_g_ref, ln_b_ref,
                 w_out_ref, gffn_ref, w_upg_ref,
                 glu_o, k_o, v_o, gate_o,
                 hb_ref, z_ref, glu_buf, mix_ref):
    l = N_META
    _in_proj(x_ref, gmix_ref, w_in_ref, hb_ref, z_ref, 1, l, l)
    glu_buf[0:HALO, :] = jnp.zeros((HALO, D_CONV), F32)
    _run(_conv_items(z_ref, 0, l, glu_buf, cw_ref, dw_b_ref, ln_g_ref, ln_b_ref, mix_ref, l))
    glu_o[...] = glu_buf[HALO:HALO + l, :]
    k = z_ref[:, _K0:_K0 + D_KV]
    v = z_ref[:, _V0:_V0 + D_KV]
    k_o[...] = k
    v_o[...] = v
    zb = jnp.zeros((BAND, D_KV), BF16)
    zp = jnp.zeros((NKEY - BAND - N_META, D_KV), BF16)
    kb = jnp.concatenate([zb, k.astype(BF16), zp], axis=0)
    vb = jnp.concatenate([zb, v.astype(BF16), zp], axis=0)
    _run(_attention_items(1, l, lambda b: z_ref[:, _Q0:_Q0 + D_ATTN], lambda b: kb, lambda b: vb,
                          lambda b, g: bias_ref[0, g], sinks_ref, _put_heads(mix_ref, lambda b: 0, l)))
    x1 = x_ref[0] + _dot(mix_ref[...], w_out_ref[...])
    h2 = _rms(x1, gffn_ref[...]).astype(BF16)
    gate = _dot(h2, w_upg_ref[...])
    gate_o[...] = gate[N_META - 8:N_META, :]


def _run_meta(sinks, meta_tokens, bias_m, gmix, w_in, cw, dw_b, ln_g, ln_b, w_out, gffn, w_up):
    l = N_META
    full = lambda a: pl.BlockSpec(a.shape, lambda i: (0,) * a.ndim)
    x = meta_tokens[None]
    ins = [x, bias_m, gmix, w_in, cw, dw_b, ln_g, ln_b, w_out, gffn]
    in_specs = [pl.BlockSpec(memory_space=pltpu.SMEM)] + [full(a) for a in ins]
    in_specs.append(pl.BlockSpec((D_MODEL, D_FF), lambda i: (0, 0)))
    out_shape = [jax.ShapeDtypeStruct((l, D_CONV), F32), jax.ShapeDtypeStruct((l, D_KV), F32),
                 jax.ShapeDtypeStruct((l, D_KV), F32), jax.ShapeDtypeStruct((8, D_FF), F32)]
    return pl.pallas_call(
        _meta_kernel,
        grid=(1,),
        out_shape=out_shape,
        in_specs=in_specs,
        out_specs=[pl.BlockSpec(s.shape, lambda i: (0, 0)) for s in out_shape],
        scratch_shapes=[pltpu.VMEM((l, D_MODEL), BF16), pltpu.VMEM((l, D_IN), F32),
                        pltpu.VMEM((HALO + l, D_CONV), F32), pltpu.VMEM((l, D_MODEL), BF16)],
        compiler_params=pltpu.CompilerParams(vmem_limit_bytes=VMEM_LIMIT),
        name="meta_tokens",
    )(sinks, *ins, w_up)


def _prompt_kernel(sinks_ref, x_ref, glu_m_ref, k_m_ref, v_m_ref, gate_m_ref, bias_ref,
                   gmix_ref, w_in_ref, cw_ref, dw_b_ref, ln_g_ref, ln_b_ref, w_out_ref, gffn_ref,
                   w_up_ref, fw_ref, fb_ref, w_down_ref, gfin_ref,
                   y_o, kwin_o, vwin_o, conva_o, ffn_o,
                   hb_ref, z_ref, glu_buf, kv_buf, mix_ref, x1_ref, gbuf, ubuf, yh_ref, gcarry):
    t = pl.program_id(1)
    l = TM
    rc = 128

    @pl.when(t == 0)
    def _():
        glu_buf[0:HALO - N_META, :] = jnp.zeros((HALO - N_META, D_CONV), F32)
        glu_buf[HALO - N_META:HALO, :] = glu_m_ref[...]
        kwin_o[0] = jnp.zeros((WINDOW, D_KV), F32)
        vwin_o[0] = jnp.zeros((WINDOW, D_KV), F32)
        gcarry[...] = gate_m_ref[...]

    _in_proj(x_ref, gmix_ref, w_in_ref, hb_ref, z_ref, 1, l, rc)

    conv_items = _conv_items(z_ref, 0, l, glu_buf, cw_ref, dw_b_ref, ln_g_ref, ln_b_ref, mix_ref, rc)

    def kv_rows():
        for idx, (win_o, c0) in enumerate(((kwin_o, _K0), (vwin_o, _V0))):
            kv_buf[idx, 0:WINDOW, :] = win_o[0].astype(BF16)
            kv_buf[idx, WINDOW:WINDOW + l, :] = z_ref[:, c0:c0 + D_KV].astype(BF16)
            win_o[0] = z_ref[l - WINDOW:l, c0:c0 + D_KV]

    zp = jnp.zeros((NKEY - BAND - N_META, D_KV), BF16)
    km = jnp.concatenate([k_m_ref[...].astype(BF16), zp], axis=0)
    vm = jnp.concatenate([v_m_ref[...].astype(BF16), zp], axis=0)

    def variant(c):
        return jnp.where(t == 0, c, 3) if c < 3 else 3

    rows = lambda c: c * CHUNK
    attn_items = [kv_rows] + _attention_items(
        l // CHUNK, CHUNK,
        lambda c: z_ref[rows(c):rows(c) + CHUNK, _Q0:_Q0 + D_ATTN],
        lambda c: jnp.concatenate([kv_buf[0, rows(c):rows(c) + BAND, :], km], axis=0),
        lambda c: jnp.concatenate([kv_buf[1, rows(c):rows(c) + BAND, :], vm], axis=0),
        lambda c, g: bias_ref[variant(c), g],
        sinks_ref, _put_heads(mix_ref, rows, CHUNK))

    _run(_interleave(conv_items, attn_items))
    conva_o[0] = glu_buf[HALO + l - (CONV_WIDTH - 1):HALO + l, :]
    glu_buf[0:HALO, :] = glu_buf[l:l + HALO, :]

    _out_proj(x_ref, mix_ref, w_out_ref, gffn_ref, x1_ref, hb_ref, 1, l, 256)

    def put_halo(gb, s, cols):
        gb[s, 0:8, :] = gcarry[:, cols]

    def save_carry(gb, s, cols):
        gcarry[:, cols] = gb[s, l:l + 8, :]

    _ffn_hidden(hb_ref, w_up_ref, fw_ref, fb_ref, gbuf, ubuf, yh_ref, put_halo, save_carry, 1, l, rc)
    ffn_o[0] = gcarry[8 - (FFN_CONV_WIDTH - 1):8, :]
    _ffn_out(x1_ref, yh_ref, w_down_ref, gfin_ref, y_o, 1, l, 256)


def _const_spec(a):
    return pl.BlockSpec(a.shape, lambda *_: (0,) * a.ndim, pipeline_mode=pl.Buffered(1))


def _run_prompt(sinks, x, glu_m, k_m, v_m, gate_m, bias_p, weights):
    bsz, seq, _ = x.shape
    assert seq % TM == 0 and TM % CHUNK == 0 and TM >= WINDOW
    consts = [glu_m, k_m, v_m, gate_m, bias_p, *weights]
    in_specs = ([pl.BlockSpec(memory_space=pltpu.SMEM), pl.BlockSpec((1, TM, D_MODEL), lambda b, t: (b, t, 0))]
                + [_const_spec(a) for a in consts])
    out_shape = [jax.ShapeDtypeStruct((bsz, seq, D_MODEL), F32),
                 jax.ShapeDtypeStruct((bsz, WINDOW, D_KV), F32), jax.ShapeDtypeStruct((bsz, WINDOW, D_KV), F32),
                 jax.ShapeDtypeStruct((bsz, CONV_WIDTH - 1, D_CONV), F32),
                 jax.ShapeDtypeStruct((bsz, FFN_CONV_WIDTH - 1, D_FF), F32)]
    out_specs = [pl.BlockSpec((1, TM, D_MODEL), lambda b, t: (b, t, 0))] + [
        pl.BlockSpec((1,) + s.shape[1:], lambda b, t: (b, 0, 0)) for s in out_shape[1:]]
    scratch = [pltpu.VMEM((TM, D_MODEL), BF16), pltpu.VMEM((TM, D_IN), F32),
               pltpu.VMEM((HALO + TM, D_CONV), F32), pltpu.VMEM((2, WINDOW + TM, D_KV), BF16),
               pltpu.VMEM((TM, D_MODEL), BF16), pltpu.VMEM((TM, D_MODEL), F32),
               pltpu.VMEM((2, 1, 8 + TM, FFN_CW), F32), pltpu.VMEM((2, TM, FFN_CW), F32),
               pltpu.VMEM((TM, D_FF), BF16), pltpu.VMEM((8, D_FF), F32)]
    return pl.pallas_call(
        _prompt_kernel,
        grid=(bsz, seq // TM),
        out_shape=out_shape,
        in_specs=in_specs,
        out_specs=out_specs,
        scratch_shapes=scratch,
        compiler_params=pltpu.CompilerParams(dimension_semantics=("arbitrary", "arbitrary"),
                                             vmem_limit_bytes=VMEM_LIMIT),
        name="prompt_step",
    )(sinks, x, *consts)


def _sample_kernel(sinks_ref, x_ref, kmeta_ref, vmeta_ref, kwin_ref, vwin_ref, conva_ref, ffn_ref, bias_ref,
                   gmix_ref, w_in_ref, cw_ref, dw_b_ref, ln_g_ref, ln_b_ref, w_out_ref, gffn_ref,
                   w_up_ref, fw_ref, fb_ref, w_down_ref, gfin_ref,
                   y_o, kwin_o, vwin_o, conva_o, ffn_o,
                   hb_ref, z_ref, glu_buf, mix_ref, x1_ref, gbuf, ubuf, yh_ref):
    ns, l, _ = x_ref.shape
    _in_proj(x_ref, gmix_ref, w_in_ref, hb_ref, z_ref, ns, l, l)
    glu_buf[0:HALO - 30, :] = jnp.zeros((HALO - 30, D_CONV), F32)
    zgap = jnp.zeros((BAND - WINDOW - l, D_KV), BF16)
    zp = jnp.zeros((NKEY - BAND - N_META, D_KV), BF16)
    def load_state(s):
        glu_buf[HALO - (CONV_WIDTH - 1):HALO, :] = conva_ref[s]

    def save_state(s):
        conva_o[s] = glu_buf[HALO + l - (CONV_WIDTH - 1):HALO + l, :]
        for win_ref, win_o, c0 in ((kwin_ref, kwin_o, _K0), (vwin_ref, vwin_o, _V0)):
            win_o[s, 0:WINDOW - l, :] = win_ref[s, l:WINDOW, :]
            win_o[s, WINDOW - l:WINDOW, :] = z_ref[s * l:(s + 1) * l, c0:c0 + D_KV]

    conv_items = []
    for s in range(ns):
        conv_items += ([functools.partial(load_state, s)]
                       + _conv_items(z_ref, s * l, l, glu_buf, cw_ref, dw_b_ref, ln_g_ref, ln_b_ref, mix_ref, l)
                       + [functools.partial(save_state, s)])

    def key_slots(win_ref, meta_ref, c0):
        def get(s):
            new = z_ref[s * l:(s + 1) * l, c0:c0 + D_KV]
            return jnp.concatenate([win_ref[s].astype(BF16), new.astype(BF16), zgap,
                                    meta_ref[s].astype(BF16), zp], axis=0)
        return get

    rows = lambda s: s * l
    attn_items = _attention_items(ns, l, lambda s: z_ref[rows(s):rows(s) + l, _Q0:_Q0 + D_ATTN],
                                  key_slots(kwin_ref, kmeta_ref, _K0), key_slots(vwin_ref, vmeta_ref, _V0),
                                  lambda s, g: bias_ref[0, g], sinks_ref, _put_heads(mix_ref, rows, l))
    _run(_interleave(conv_items, attn_items))

    _out_proj(x_ref, mix_ref, w_out_ref, gffn_ref, x1_ref, hb_ref, ns, l, 256)

    nprev = FFN_CONV_WIDTH - 1
    for i in range(2):
        for s in range(ns):
            gbuf[i, s, 0:8, :] = jnp.zeros((8, FFN_CW), F32)

    def put_halo(gb, s, cols):
        gb[s, 8 - nprev:8, :] = ffn_ref[s, :, cols]

    def save_carry(gb, s, cols):
        ffn_o[s, :, cols] = gb[s, 8 + l - nprev:8 + l, :]

    _ffn_hidden(hb_ref, w_up_ref, fw_ref, fb_ref, gbuf, ubuf, yh_ref, put_halo, save_carry, ns, l, l)
    _ffn_out(x1_ref, yh_ref, w_down_ref, gfin_ref, y_o, ns, l, 256)


def _run_sample(sinks, x, kmeta, vmeta, kwin, vwin, conva, ffn, bias_s, weights):
    nstream, l, _ = x.shape
    assert nstream % SB == 0 and l % 16 == 0 and l <= CHUNK and kwin.shape[1] == WINDOW
    rows = SB * l
    per_stream = [x, kmeta, vmeta, kwin, vwin, conva, ffn]
    consts = [bias_s, *weights]
    in_specs = ([pl.BlockSpec(memory_space=pltpu.SMEM)]
                + [pl.BlockSpec((SB,) + a.shape[1:], lambda i: (i, 0, 0)) for a in per_stream]
                + [_const_spec(a) for a in consts])
    out_shape = [jax.ShapeDtypeStruct(a.shape, F32) for a in (x, kwin, vwin, conva, ffn)]
    out_specs = [pl.BlockSpec((SB,) + s.shape[1:], lambda i: (i, 0, 0)) for s in out_shape]
    scratch = [pltpu.VMEM((rows, D_MODEL), BF16), pltpu.VMEM((rows, D_IN), F32),
               pltpu.VMEM((HALO + l, D_CONV), F32),
               pltpu.VMEM((rows, D_MODEL), BF16), pltpu.VMEM((rows, D_MODEL), F32),
               pltpu.VMEM((2, SB, 8 + l, FFN_CW), F32), pltpu.VMEM((2, rows, FFN_CW), F32),
               pltpu.VMEM((rows, D_FF), BF16)]
    return pl.pallas_call(
        _sample_kernel,
        grid=(nstream // SB,),
        out_shape=out_shape,
        in_specs=in_specs,
        out_specs=out_specs,
        scratch_shapes=scratch,
        compiler_params=pltpu.CompilerParams(dimension_semantics=("arbitrary",), vmem_limit_bytes=VMEM_LIMIT),
        name="sample_step",
    )(sinks, *per_stream, *consts)


def kernel(x_prompt, x_sample, cache_k_meta, cache_v_meta, cache_k_win, cache_v_win, state_conv_a, state_conv_ffn,
           meta_tokens, rel_bias_table, norm_mix, w_in, conv_dw_w, conv_dw_b, conv_ln_g, conv_ln_b, attn_sinks,
           w_out, norm_ffn, w_up, ffn_dw_w, ffn_dw_b, w_down, norm_final):
    depth = w_in.shape[0]
    assert depth == 1, "single-layer step"
    bsz = x_prompt.shape[0]
    nstream, dec_seq, _ = x_sample.shape
    win = cache_k_win.shape[2]
    assert win == WINDOW

    bp, bs, bm = _bucket_tables(PAST_LEN, dec_seq)
    bias_p, bias_s, bias_m = _build_bias(rel_bias_table, jnp.asarray(bp), jnp.asarray(bs), jnp.asarray(bm))

    row = lambda a: a.reshape(1, -1)
    w_in_b = w_in[0].astype(BF16)
    w_out_b = w_out[0].astype(BF16)
    w_up_b = w_up[0].astype(BF16)
    w_down_b = w_down[0].astype(BF16)
    sinks = attn_sinks[0]
    gmix, gffn, gfin = row(norm_mix[0]), row(norm_ffn[0]), row(norm_final)
    cw = jnp.broadcast_to(conv_dw_w[0][::-1][:, None, :], (CONV_WIDTH, 8, D_CONV))
    dw_b = row(conv_dw_b[0])
    ln_g, ln_b = row(conv_ln_g[0]), row(conv_ln_b[0])
    fw = jnp.broadcast_to(ffn_dw_w[0][:, None, :], (FFN_CONV_WIDTH, 8, D_FF))
    fb = jnp.broadcast_to(ffn_dw_b[0][None, :], (8, D_FF))

    glu_m, k_m, v_m, gate_m = _run_meta(sinks, meta_tokens, bias_m, gmix, w_in_b, cw, dw_b, ln_g, ln_b,
                                        w_out_b, gffn, w_up_b)

    weights = [gmix, w_in_b, cw, dw_b, ln_g, ln_b, w_out_b, gffn, w_up_b, fw, fb, w_down_b, gfin]
    y_p, kwin_p, vwin_p, conva_p, ffn_p = _run_prompt(sinks, x_prompt, glu_m, k_m, v_m, gate_m, bias_p, weights)

    kv = lambda a: a.reshape(a.shape[0], a.shape[1], D_KV)
    y_s, kwin_s, vwin_s, conva_s, ffn_s = _run_sample(
        sinks, x_sample, kv(cache_k_meta[0]), kv(cache_v_meta[0]), kv(cache_k_win[0]), kv(cache_v_win[0]),
        state_conv_a[0], state_conv_ffn[0], bias_s, weights)

    heads = lambda a: a.reshape(1, a.shape[0], a.shape[1], N_KV_HEADS, HEAD_DIM)
    meta_b = lambda a: jnp.broadcast_to(a.reshape(1, 1, N_META, N_KV_HEADS, HEAD_DIM),
                                        (1, bsz, N_META, N_KV_HEADS, HEAD_DIM))
    return (y_p, y_s, meta_b(k_m), meta_b(v_m), heads(kwin_p), heads(vwin_p), conva_p[None], ffn_p[None],
            heads(kwin_s), heads(vwin_s), conva_s[None], ffn_s[None])
```

```python
import functools
import math

import numpy as np
import jax
import jax.numpy as jnp
from jax import lax
from jax.experimental import pallas as pl
from jax.experimental.pallas import tpu as pltpu

F32 = jnp.float32
BF16 = jnp.bfloat16

D_MODEL = 1024
CHUNK = 64
N_META = 16
D_CONV = 512
CONV_WIDTH = 31
N_HEADS = 8
N_KV_HEADS = 2
HEAD_DIM = 64
HEADS_PER_KV = N_HEADS // N_KV_HEADS
D_ATTN = N_HEADS * HEAD_DIM
D_KV = N_KV_HEADS * HEAD_DIM
WINDOW = 128
PAST_LEN = 4096
N_BUCKETS = 32
MAX_DISTANCE = 256
D_FF = 2816
FFN_CONV_WIDTH = 3
D_IN = 2 * D_CONV + D_ATTN + 2 * D_KV
EPS = 1e-6
NEG_INF = -1e30
SCALE = HEAD_DIM ** -0.5

_Q0 = 2 * D_CONV
_K0 = _Q0 + D_ATTN
_V0 = _K0 + D_KV

NKEY = 256
BAND = WINDOW + CHUNK
HALO = 32
FFN_CW = 256
TM = 512
SB = 16
VMEM_LIMIT = 60 * 1024 * 1024


def _np_bucket(rel):
    nb = N_BUCKETS // 2
    max_exact = nb // 2
    ret = np.where(rel > 0, nb, 0)
    n = np.abs(rel)
    nf = np.maximum(n, 1).astype(np.float32)
    large = max_exact + (np.log(nf / np.float32(max_exact)) / np.float32(math.log(MAX_DISTANCE / max_exact))
                         * np.float32(nb - max_exact)).astype(np.int32)
    large = np.minimum(large, nb - 1)
    return (ret + np.where(n < max_exact, n, large)).astype(np.int32)


def _bucket_tables(past_len, dec_seq):
    i = np.arange(CHUNK)[:, None]
    j = np.arange(BAND)[None, :]
    m = np.arange(N_META)[None, :]
    band = _np_bucket(j - WINDOW - i)
    pad = np.full((CHUNK, NKEY - BAND - N_META), -1, np.int32)

    def meta_for(c):
        return _np_bucket(m - (N_META + CHUNK * c + i))

    prompt = []
    for c in range(4):
        b = np.where(c - 2 + j // CHUNK >= 0, band, -1)
        prompt.append(np.concatenate([b, meta_for(c), pad], axis=1))
    assert np.array_equal(meta_for(3), meta_for(4096))
    prompt = np.stack(prompt).astype(np.int32)

    i_s = np.arange(dec_seq)[:, None]
    band_s = np.where(j < WINDOW + dec_seq, _np_bucket(j - WINDOW - i_s), -1)
    meta_s = _np_bucket(m - (N_META + past_len + i_s))
    sample = np.concatenate([band_s, meta_s, pad[:dec_seq]], axis=1)[None].astype(np.int32)

    i_m = np.arange(N_META)[:, None]
    meta_self = np.concatenate([np.full((N_META, BAND), -1, np.int32), _np_bucket(m - i_m), pad[:N_META]],
                               axis=1)[None].astype(np.int32)
    return prompt, sample, meta_self


def _bias_kernel(table_ref, bp_ref, bs_ref, bm_ref, op_ref, os_ref, om_ref):
    for b_ref, o_ref in ((bp_ref, op_ref), (bs_ref, os_ref), (bm_ref, om_ref)):
        nv, r, _ = b_ref.shape
        for v in range(nv):
            bk = b_ref[v]
            for g in range(N_KV_HEADS):
                for hl in range(HEADS_PER_KV):
                    acc = jnp.full((r, NKEY), NEG_INF, F32)
                    for b in range(N_BUCKETS):
                        acc = jnp.where(bk == b, table_ref[b, HEADS_PER_KV * g + hl], acc)
                    o_ref[v, g, r * hl:r * (hl + 1), :] = acc


def _build_bias(table, bp, bs, bm):
    shapes = [jax.ShapeDtypeStruct((b.shape[0], N_KV_HEADS, HEADS_PER_KV * b.shape[1], NKEY), F32)
              for b in (bp, bs, bm)]
    vm = pl.BlockSpec(memory_space=pltpu.VMEM)
    return pl.pallas_call(
        _bias_kernel,
        out_shape=shapes,
        in_specs=[pl.BlockSpec(memory_space=pltpu.SMEM), vm, vm, vm],
        out_specs=[vm, vm, vm],
        name="rel_bias",
    )(table, bp, bs, bm)


def _dot(a, b):
    return jnp.dot(a, b, preferred_element_type=F32)


def _dot_nt(a, b):
    return lax.dot_general(a, b, (((1,), (1,)), ((), ())), preferred_element_type=F32)


def _rms(x, g):
    ms = jnp.mean(x * x, axis=-1, keepdims=True)
    return (x * lax.rsqrt(ms + EPS)) * g


def _swap_halves(x):
    return pltpu.roll(x, HEAD_DIM, axis=1)


def _attention_items(nblk, r, get_q, get_k, get_v, get_bias, sinks_ref, put_out):
    def block(b):
        q = get_q(b) * SCALE
        kb = get_k(b)
        vb = get_v(b)
        lo = lax.broadcasted_iota(jnp.int32, (r, 2 * HEAD_DIM), 1) < HEAD_DIM
        for g in range(N_KV_HEADS):
            keep = lo if g == 0 else jnp.logical_not(lo)
            parts = []
            for hl in range(HEADS_PER_KV):
                h = HEADS_PER_KV * g + hl
                qh = q[:, 128 * (h // 2):128 * (h // 2) + 128]
                if h % 2 != g:
                    qh = _swap_halves(qh)
                parts.append(jnp.where(keep, qh, 0.0))
            qs = jnp.concatenate(parts, axis=0).astype(BF16)
            s = _dot_nt(qs, kb) + get_bias(b, g)
            probs = []
            for hl in range(HEADS_PER_KV):
                sink = sinks_ref[HEADS_PER_KV * g + hl]
                seg = s[r * hl:r * (hl + 1), :]
                mx = jnp.maximum(jnp.max(seg, axis=-1, keepdims=True), sink)
                e = jnp.exp(seg - mx)
                den = jnp.sum(e, axis=-1, keepdims=True) + jnp.exp(sink - mx)
                probs.append((e * (1.0 / den)).astype(BF16))
            o = _dot(jnp.concatenate(probs, axis=0), vb)
            for pp in range(2):
                even = o[r * 2 * pp:r * (2 * pp + 1)]
                odd = o[r * (2 * pp + 1):r * (2 * pp + 2)]
                if g == 1:
                    even = _swap_halves(even)
                else:
                    odd = _swap_halves(odd)
                put_out(b, g, pp, jnp.where(lo, even, odd))

    return [functools.partial(block, b) for b in range(nblk)]


def _put_heads(mix_ref, row0, r):
    def put(b, g, pp, o):
        base = D_CONV + 256 * g + 128 * pp
        mix_ref[row0(b):row0(b) + r, base:base + 128] = o.astype(BF16)
    return put


def _in_proj(x_ref, gmix_ref, w_in_ref, hb_ref, z_ref, ns, l, rc):
    g = gmix_ref[...]
    for s in range(ns):
        for r0 in range(0, l, rc):
            hb_ref[s * l + r0:s * l + r0 + rc, :] = _rms(x_ref[s, r0:r0 + rc, :], g).astype(BF16)
    z_ref[:, 0:_Q0] = _dot(hb_ref[...], w_in_ref[:, 0:_Q0])
    z_ref[:, _Q0:D_IN] = _dot(hb_ref[...], w_in_ref[:, _Q0:D_IN])


def _delay(x):
    return pltpu.roll(x, 1, axis=0)


def _conv31_block(glu_buf, u0, rc, cw_ref, l0, n):
    nb = (rc + 8) // 8
    acc = None
    for r in range(7, -1, -1):
        s = None
        for a in range(4):
            d = 8 * a + r
            if d >= CONV_WIDTH:
                continue
            blk = glu_buf[u0 - 8 - 8 * a:u0 + rc - 8 * a, l0:l0 + n].reshape(nb, 8, n)
            term = blk * cw_ref[d, :, l0:l0 + n][None]
            s = term if s is None else s + term
        s = s.reshape(rc + 8, n)
        acc = s if acc is None else s + _delay(acc)
    return acc[8:]


def _run(items):
    for item in items:
        item()


def _interleave(a, b):
    out, ia, ib = [], 0, 0
    while ia < len(a) or ib < len(b):
        if ib >= len(b) or (ia < len(a) and ia * len(b) <= ib * len(a)):
            out.append(a[ia])
            ia += 1
        else:
            out.append(b[ib])
            ib += 1
    return out


def _conv_items(z_ref, row0, l, glu_buf, cw_ref, dw_b_ref, ln_g_ref, ln_b_ref, mix_ref, rc):
    rn = min(rc, 64)

    def glu(r0):
        a = z_ref[row0 + r0:row0 + r0 + rc, 0:D_CONV]
        b = z_ref[row0 + r0:row0 + r0 + rc, D_CONV:2 * D_CONV]
        glu_buf[HALO + r0:HALO + r0 + rc, :] = a * jax.nn.sigmoid(b)

    def conv(r0):
        for l0 in range(0, D_CONV, 128):
            z_ref[row0 + r0:row0 + r0 + rc, l0:l0 + 128] = _conv31_block(glu_buf, HALO + r0, rc, cw_ref, l0, 128)

    def norm(r0):
        for q0 in range(r0, r0 + rc, rn):
            acc = z_ref[row0 + q0:row0 + q0 + rn, 0:D_CONV] + dw_b_ref[...]
            mu = jnp.mean(acc, axis=-1, keepdims=True)
            d = acc - mu
            var = jnp.mean(d * d, axis=-1, keepdims=True)
            yn = (d * lax.rsqrt(var + EPS)) * ln_g_ref[...] + ln_b_ref[...]
            mix_ref[row0 + q0:row0 + q0 + rn, 0:D_CONV] = (yn * jax.nn.sigmoid(yn)).astype(BF16)

    items = []
    for r0 in range(0, l, rc):
        items += [functools.partial(glu, r0), functools.partial(conv, r0), functools.partial(norm, r0)]
    return items


def _seq_pieces(m0, mc, l):
    pc = min(l, mc)
    return [(p0,) + divmod(m0 + p0, l) + (pc,) for p0 in range(0, mc, pc)]


def _out_proj(x_ref, mix_ref, w_out_ref, gffn_ref, x1_ref, hb_ref, ns, l, mc):
    g = gffn_ref[...]
    for m0 in range(0, ns * l, mc):
        acc = _dot(mix_ref[m0:m0 + mc, :], w_out_ref[...])
        for p0, s, r0, pc in _seq_pieces(m0, mc, l):
            x1 = x_ref[s, r0:r0 + pc, :] + acc[p0:p0 + pc]
            x1_ref[m0 + p0:m0 + p0 + pc, :] = x1
            hb_ref[m0 + p0:m0 + p0 + pc, :] = _rms(x1, g).astype(BF16)


def _ffn_hidden(hb_ref, w_up_ref, fw_ref, fb_ref, gbuf, ubuf, y_ref, put_halo, save_carry, ns, l, rc):
    nch = D_FF // FFN_CW
    nb = (rc + 8) // 8

    def matmuls(ci):
        c = ci * FFN_CW
        gb = gbuf.at[ci % 2]
        ug = _dot(hb_ref[...], w_up_ref[:, c:c + FFN_CW])
        for s in range(ns):
            put_halo(gb, s, slice(c, c + FFN_CW))
            gb[s, 8:8 + l, :] = ug[s * l:(s + 1) * l]
        ubuf[ci % 2] = _dot(hb_ref[...], w_up_ref[:, D_FF + c:D_FF + c + FFN_CW])

    def elementwise(ci):
        c = ci * FFN_CW
        gb = gbuf.at[ci % 2]
        ub = ubuf.at[ci % 2]
        cols = slice(c, c + FFN_CW)
        w0 = fw_ref[0, :, cols][None]
        w1 = fw_ref[1, :, cols][None]
        w2 = fw_ref[2, :, cols][None]
        bb = fb_ref[:, cols][None]
        for s in range(ns):
            for r0 in range(0, l, rc):
                g = gb[s, r0:r0 + rc + 8, :].reshape(nb, 8, FFN_CW)
                t = (g * w0).reshape(rc + 8, FFN_CW)
                t = (g * w1).reshape(rc + 8, FFN_CW) + _delay(t)
                t = (g * w2 + bb).reshape(rc + 8, FFN_CW) + _delay(t)
                cv = t[8:]
                y = (cv * jax.nn.sigmoid(cv)) * ub[s * l + r0:s * l + r0 + rc, :]
                y_ref[s * l + r0:s * l + r0 + rc, cols] = y.astype(BF16)
            save_carry(gb, s, cols)

    matmuls(0)
    for ci in range(nch):
        if ci + 1 < nch:
            matmuls(ci + 1)
        elementwise(ci)


def _ffn_out(x1_ref, y_ref, w_down_ref, gfin_ref, o_ref, ns, l, mc):
    g = gfin_ref[...]
    for m0 in range(0, ns * l, mc):
        acc = _dot(y_ref[m0:m0 + mc, :], w_down_ref[...])
        for p0, s, r0, pc in _seq_pieces(m0, mc, l):
            x2 = x1_ref[m0 + p0:m0 + p0 + pc, :] + acc[p0:p0 + pc]
            o_ref[s, r0:r0 + pc, :] = _rms(x2, g)


def _meta_kernel(sinks_ref, x_ref, bias_ref, gmix_ref, w_in_ref, cw_ref, dw_b_ref, ln_g_ref, ln_b_ref,
                 w_out_ref, gffn_ref, w_upg_ref,
                 glu_o, k_o, v_o, gate_o,
                 hb_ref, z_ref, glu_buf, mix_ref):
    l = N_META
    _in_proj(x_ref, gmix_ref, w_in_ref, hb_ref, z_ref, 1, l, l)
    glu_buf[0:HALO, :] = jnp.zeros((HALO, D_CONV), F32)
    _run(_conv_items(z_ref, 0, l, glu_buf, cw_ref, dw_b_ref, ln_g_ref, ln_b_ref, mix_ref, l))
    glu_o[...] = glu_buf[HALO:HALO + l, :]
    k = z_ref[:, _K0:_K0 + D_KV]
    v = z_ref[:, _V0:_V0 + D_KV]
    k_o[...] = k
    v_o[...] = v
    zb = jnp.zeros((BAND, D_KV), BF16)
    zp = jnp.zeros((NKEY - BAND - N_META, D_KV), BF16)
    kb = jnp.concatenate([zb, k.astype(BF16), zp], axis=0)
    vb = jnp.concatenate([zb, v.astype(BF16), zp], axis=0)
    _run(_attention_items(1, l, lambda b: z_ref[:, _Q0:_Q0 + D_ATTN], lambda b: kb, lambda b: vb,
                          lambda b, g: bias_ref[0, g], sinks_ref, _put_heads(mix_ref, lambda b: 0, l)))
    x1 = x_ref[0] + _dot(mix_ref[...], w_out_ref[...])
    h2 = _rms(x1, gffn_ref[...]).astype(BF16)
    gate = _dot(h2, w_upg_ref[...])
    gate_o[...] = gate[N_META - 8:N_META, :]


def _run_meta(sinks, meta_tokens, bias_m, gmix, w_in, cw, dw_b, ln_g, ln_b, w_out, gffn, w_up):
    l = N_META
    full = lambda a: pl.BlockSpec(a.shape, lambda i: (0,) * a.ndim)
    x = meta_tokens[None]
    ins = [x, bias_m, gmix, w_in, cw, dw_b, ln_g, ln_b, w_out, gffn]
    in_specs = [pl.BlockSpec(memory_space=pltpu.SMEM)] + [full(a) for a in ins]
    in_specs.append(pl.BlockSpec((D_MODEL, D_FF), lambda i: (0, 0)))
    out_shape = [jax.ShapeDtypeStruct((l, D_CONV), F32), jax.ShapeDtypeStruct((l, D_KV), F32),
                 jax.ShapeDtypeStruct((l, D_KV), F32), jax.ShapeDtypeStruct((8, D_FF), F32)]
    return pl.pallas_call(
        _meta_kernel,
        grid=(1,),
        out_shape=out_shape,
        in_specs=in_specs,
        out_specs=[pl.BlockSpec(s.shape, lambda i: (0, 0)) for s in out_shape],
        scratch_shapes=[pltpu.VMEM((l, D_MODEL), BF16), pltpu.VMEM((l, D_IN), F32),
                        pltpu.VMEM((HALO + l, D_CONV), F32), pltpu.VMEM((l, D_MODEL), BF16)],
        compiler_params=pltpu.CompilerParams(vmem_limit_bytes=VMEM_LIMIT),
        name="meta_tokens",
    )(sinks, *ins, w_up)


def _prompt_kernel(sinks_ref, x_ref, glu_m_ref, k_m_ref, v_m_ref, gate_m_ref, bias_ref,
                   gmix_ref, w_in_ref, cw_ref, dw_b_ref, ln_g_ref, ln_b_ref, w_out_ref, gffn_ref,
                   w_up_ref, fw_ref, fb_ref, w_down_ref, gfin_ref,
                   y_o, kwin_o, vwin_o, conva_o, ffn_o,
                   hb_ref, z_ref, glu_buf, kv_buf, mix_ref, x1_ref, gbuf, ubuf, yh_ref, gcarry):
    t = pl.program_id(1)
    l = TM
    rc = 128

    @pl.when(t == 0)
    def _():
        glu_buf[0:HALO - N_META, :] = jnp.zeros((HALO - N_META, D_CONV), F32)
        glu_buf[HALO - N_META:HALO, :] = glu_m_ref[...]
        kwin_o[0] = jnp.zeros((WINDOW, D_KV), F32)
        vwin_o[0] = jnp.zeros((WINDOW, D_KV), F32)
        gcarry[...] = gate_m_ref[...]

    def in_rows(r0):
        hb_ref[r0:r0 + rc, :] = _rms(x_ref[0, r0:r0 + rc, :], gmix_ref[...]).astype(BF16)
        z_ref[r0:r0 + rc, 0:_Q0] = _dot(hb_ref[r0:r0 + rc, :], w_in_ref[:, 0:_Q0])
        z_ref[r0:r0 + rc, _Q0:D_IN] = _dot(hb_ref[r0:r0 + rc, :], w_in_ref[:, _Q0:D_IN])

    conv_items = _conv_items(z_ref, 0, l, glu_buf, cw_ref, dw_b_ref, ln_g_ref, ln_b_ref, mix_ref, rc)
    nrc = l // rc
    per = len(conv_items) // nrc
    in_rows(0)
    for j in range(nrc):
        if j + 1 < nrc:
            in_rows((j + 1) * rc)
        _run(conv_items[per * j:per * (j + 1)])

    def kv_rows():
        for idx, (win_o, c0) in enumerate(((kwin_o, _K0), (vwin_o, _V0))):
            kv_buf[idx, 0:WINDOW, :] = win_o[0].astype(BF16)
            kv_buf[idx, WINDOW:WINDOW + l, :] = z_ref[:, c0:c0 + D_KV].astype(BF16)
            win_o[0] = z_ref[l - WINDOW:l, c0:c0 + D_KV]

    zp = jnp.zeros((NKEY - BAND - N_META, D_KV), BF16)
    km = jnp.concatenate([k_m_ref[...].astype(BF16), zp], axis=0)
    vm = jnp.concatenate([v_m_ref[...].astype(BF16), zp], axis=0)

    def variant(c):
        return jnp.where(t == 0, c, 3) if c < 3 else 3

    rows = lambda c: c * CHUNK
    attn_items = [kv_rows] + _attention_items(
        l // CHUNK, CHUNK,
        lambda c: z_ref[rows(c):rows(c) + CHUNK, _Q0:_Q0 + D_ATTN],
        lambda c: jnp.concatenate([kv_buf[0, rows(c):rows(c) + BAND, :], km], axis=0),
        lambda c: jnp.concatenate([kv_buf[1, rows(c):rows(c) + BAND, :], vm], axis=0),
        lambda c, g: bias_ref[variant(c), g],
        sinks_ref, _put_heads(mix_ref, rows, CHUNK))

    _run(attn_items)
    conva_o[0] = glu_buf[HALO + l - (CONV_WIDTH - 1):HALO + l, :]
    glu_buf[0:HALO, :] = glu_buf[l:l + HALO, :]

    _out_proj(x_ref, mix_ref, w_out_ref, gffn_ref, x1_ref, hb_ref, 1, l, 256)

    def put_halo(gb, s, cols):
        gb[s, 0:8, :] = gcarry[:, cols]

    def save_carry(gb, s, cols):
        gcarry[:, cols] = gb[s, l:l + 8, :]

    _ffn_hidden(hb_ref, w_up_ref, fw_ref, fb_ref, gbuf, ubuf, yh_ref, put_halo, save_carry, 1, l, rc)
    ffn_o[0] = gcarry[8 - (FFN_CONV_WIDTH - 1):8, :]
    _ffn_out(x1_ref, yh_ref, w_down_ref, gfin_ref, y_o, 1, l, 256)


def _const_spec(a):
    return pl.BlockSpec(a.shape, lambda *_: (0,) * a.ndim, pipeline_mode=pl.Buffered(1))


def _run_prompt(sinks, x, glu_m, k_m, v_m, gate_m, bias_p, weights):
    bsz, seq, _ = x.shape
    assert seq % TM == 0 and TM % CHUNK == 0 and TM >= WINDOW
    consts = [glu_m, k_m, v_m, gate_m, bias_p, *weights]
    in_specs = ([pl.BlockSpec(memory_space=pltpu.SMEM), pl.BlockSpec((1, TM, D_MODEL), lambda b, t: (b, t, 0))]
                + [_const_spec(a) for a in consts])
    out_shape = [jax.ShapeDtypeStruct((bsz, seq, D_MODEL), F32),
                 jax.ShapeDtypeStruct((bsz, WINDOW, D_KV), F32), jax.ShapeDtypeStruct((bsz, WINDOW, D_KV), F32),
                 jax.ShapeDtypeStruct((bsz, CONV_WIDTH - 1, D_CONV), F32),
                 jax.ShapeDtypeStruct((bsz, FFN_CONV_WIDTH - 1, D_FF), F32)]
    out_specs = [pl.BlockSpec((1, TM, D_MODEL), lambda b, t: (b, t, 0))] + [
        pl.BlockSpec((1,) + s.shape[1:], lambda b, t: (b, 0, 0)) for s in out_shape[1:]]
    scratch = [pltpu.VMEM((TM, D_MODEL), BF16), pltpu.VMEM((TM, D_IN), F32),
               pltpu.VMEM((HALO + TM, D_CONV), F32), pltpu.VMEM((2, WINDOW + TM, D_KV), BF16),
               pltpu.VMEM((TM, D_MODEL), BF16), pltpu.VMEM((TM, D_MODEL), F32),
               pltpu.VMEM((2, 1, 8 + TM, FFN_CW), F32), pltpu.VMEM((2, TM, FFN_CW), F32),
               pltpu.VMEM((TM, D_FF), BF16), pltpu.VMEM((8, D_FF), F32)]
    return pl.pallas_call(
        _prompt_kernel,
        grid=(bsz, seq // TM),
        out_shape=out_shape,
        in_specs=in_specs,
        out_specs=out_specs,
        scratch_shapes=scratch,
        compiler_params=pltpu.CompilerParams(dimension_semantics=("arbitrary", "arbitrary"),
                                             vmem_limit_bytes=VMEM_LIMIT),
        name="prompt_step",
    )(sinks, x, *consts)


def _sample_kernel(sinks_ref, x_ref, kmeta_ref, vmeta_ref, kwin_ref, vwin_ref, conva_ref, ffn_ref, bias_ref,
                   gmix_ref, w_in_ref, cw_ref, dw_b_ref, ln_g_ref, ln_b_ref, w_out_ref, gffn_ref,
                   w_up_ref, fw_ref, fb_ref, w_down_ref, gfin_ref,
                   y_o, kwin_o, vwin_o, conva_o, ffn_o,
                   hb_ref, z_ref, glu_buf, mix_ref, x1_ref, gbuf, ubuf, yh_ref):
    ns, l, _ = x_ref.shape
    _in_proj(x_ref, gmix_ref, w_in_ref, hb_ref, z_ref, ns, l, l)
    glu_buf[0:HALO - 30, :] = jnp.zeros((HALO - 30, D_CONV), F32)
    zgap = jnp.zeros((BAND - WINDOW - l, D_KV), BF16)
    zp = jnp.zeros((NKEY - BAND - N_META, D_KV), BF16)
    def load_state(s):
        glu_buf[HALO - (CONV_WIDTH - 1):HALO, :] = conva_ref[s]

    def save_state(s):
        conva_o[s] = glu_buf[HALO + l - (CONV_WIDTH - 1):HALO + l, :]
        for win_ref, win_o, c0 in ((kwin_ref, kwin_o, _K0), (vwin_ref, vwin_o, _V0)):
            win_o[s, 0:WINDOW - l, :] = win_ref[s, l:WINDOW, :]
            win_o[s, WINDOW - l:WINDOW, :] = z_ref[s * l:(s + 1) * l, c0:c0 + D_KV]

    conv_items = []
    for s in range(ns):
        conv_items += ([functools.partial(load_state, s)]
                       + _conv_items(z_ref, s * l, l, glu_buf, cw_ref, dw_b_ref, ln_g_ref, ln_b_ref, mix_ref, l)
                       + [functools.partial(save_state, s)])

    def key_slots(win_ref, meta_ref, c0):
        def get(s):
            new = z_ref[s * l:(s + 1) * l, c0:c0 + D_KV]
            return jnp.concatenate([win_ref[s].astype(BF16), new.astype(BF16), zgap,
                                    meta_ref[s].astype(BF16), zp], axis=0)
        return get

    rows = lambda s: s * l
    attn_items = _attention_items(ns, l, lambda s: z_ref[rows(s):rows(s) + l, _Q0:_Q0 + D_ATTN],
                                  key_slots(kwin_ref, kmeta_ref, _K0), key_slots(vwin_ref, vmeta_ref, _V0),
                                  lambda s, g: bias_ref[0, g], sinks_ref, _put_heads(mix_ref, rows, l))
    _run(_interleave(conv_items, attn_items))

    _out_proj(x_ref, mix_ref, w_out_ref, gffn_ref, x1_ref, hb_ref, ns, l, 256)

    nprev = FFN_CONV_WIDTH - 1
    for i in range(2):
        for s in range(ns):
            gbuf[i, s, 0:8, :] = jnp.zeros((8, FFN_CW), F32)

    def put_halo(gb, s, cols):
        gb[s, 8 - nprev:8, :] = ffn_ref[s, :, cols]

    def save_carry(gb, s, cols):
        ffn_o[s, :, cols] = gb[s, 8 + l - nprev:8 + l, :]

    _ffn_hidden(hb_ref, w_up_ref, fw_ref, fb_ref, gbuf, ubuf, yh_ref, put_halo, save_carry, ns, l, l)
    _ffn_out(x1_ref, yh_ref, w_down_ref, gfin_ref, y_o, ns, l, 256)


def _run_sample(sinks, x, kmeta, vmeta, kwin, vwin, conva, ffn, bias_s, weights):
    nstream, l, _ = x.shape
    assert nstream % SB == 0 and l % 16 == 0 and l <= CHUNK and kwin.shape[1] == WINDOW
    rows = SB * l
    per_stream = [x, kmeta, vmeta, kwin, vwin, conva, ffn]
    consts = [bias_s, *weights]
    in_specs = ([pl.BlockSpec(memory_space=pltpu.SMEM)]
                + [pl.BlockSpec((SB,) + a.shape[1:], lambda i: (i, 0, 0)) for a in per_stream]
                + [_const_spec(a) for a in consts])
    out_shape = [jax.ShapeDtypeStruct(a.shape, F32) for a in (x, kwin, vwin, conva, ffn)]
    out_specs = [pl.BlockSpec((SB,) + s.shape[1:], lambda i: (i, 0, 0)) for s in out_shape]
    scratch = [pltpu.VMEM((rows, D_MODEL), BF16), pltpu.VMEM((rows, D_IN), F32),
               pltpu.VMEM((HALO + l, D_CONV), F32),
               pltpu.VMEM((rows, D_MODEL), BF16), pltpu.VMEM((rows, D_MODEL), F32),
               pltpu.VMEM((2, SB, 8 + l, FFN_CW), F32), pltpu.VMEM((2, rows, FFN_CW), F32),
               pltpu.VMEM((rows, D_FF), BF16)]
    return pl.pallas_call(
        _sample_kernel,
        grid=(nstream // SB,),
        out_shape=out_shape,
        in_specs=in_specs,
        out_specs=out_specs,
        scratch_shapes=scratch,
        compiler_params=pltpu.CompilerParams(dimension_semantics=("arbitrary",), vmem_limit_bytes=VMEM_LIMIT),
        name="sample_step",
    )(sinks, *per_stream, *consts)


def kernel(x_prompt, x_sample, cache_k_meta, cache_v_meta, cache_k_win, cache_v_win, state_conv_a, state_conv_ffn,
           meta_tokens, rel_bias_table, norm_mix, w_in, conv_dw_w, conv_dw_b, conv_ln_g, conv_ln_b, attn_sinks,
           w_out, norm_ffn, w_up, ffn_dw_w, ffn_dw_b, w_down, norm_final):
    depth = w_in.shape[0]
    assert depth == 1, "single-layer step"
    bsz = x_prompt.shape[0]
    nstream, dec_seq, _ = x_sample.shape
    win = cache_k_win.shape[2]
    assert win == WINDOW

    bp, bs, bm = _bucket_tables(PAST_LEN, dec_seq)
    bias_p, bias_s, bias_m = _build_bias(rel_bias_table, jnp.asarray(bp), jnp.asarray(bs), jnp.asarray(bm))

    row = lambda a: a.reshape(1, -1)
    w_in_b = w_in[0].astype(BF16)
    w_out_b = w_out[0].astype(BF16)
    w_up_b = w_up[0].astype(BF16)
    w_down_b = w_down[0].astype(BF16)
    sinks = attn_sinks[0]
    gmix, gffn, gfin = row(norm_mix[0]), row(norm_ffn[0]), row(norm_final)
    cw = jnp.broadcast_to(conv_dw_w[0][::-1][:, None, :], (CONV_WIDTH, 8, D_CONV))
    dw_b = row(conv_dw_b[0])
    ln_g, ln_b = row(conv_ln_g[0]), row(conv_ln_b[0])
    fw = jnp.broadcast_to(ffn_dw_w[0][:, None, :], (FFN_CONV_WIDTH, 8, D_FF))
    fb = jnp.broadcast_to(ffn_dw_b[0][None, :], (8, D_FF))

    glu_m, k_m, v_m, gate_m = _run_meta(sinks, meta_tokens, bias_m, gmix, w_in_b, cw, dw_b, ln_g, ln_b,
                                        w_out_b, gffn, w_up_b)

    weights = [gmix, w_in_b, cw, dw_b, ln_g, ln_b, w_out_b, gffn, w_up_b, fw, fb, w_down_b, gfin]
    y_p, kwin_p, vwin_p, conva_p, ffn_p = _run_prompt(sinks, x_prompt, glu_m, k_m, v_m, gate_m, bias_p, weights)

    kv = lambda a: a.reshape(a.shape[0], a.shape[1], D_KV)
    y_s, kwin_s, vwin_s, conva_s, ffn_s = _run_sample(
        sinks, x_sample, kv(cache_k_meta[0]), kv(cache_v_meta[0]), kv(cache_k_win[0]), kv(cache_v_win[0]),
        state_conv_a[0], state_conv_ffn[0], bias_s, weights)

    heads = lambda a: a.reshape(1, a.shape[0], a.shape[1], N_KV_HEADS, HEAD_DIM)
    meta_b = lambda a: jnp.broadcast_to(a.reshape(1, 1, N_META, N_KV_HEADS, HEAD_DIM),
                                        (1, bsz, N_META, N_KV_HEADS, HEAD_DIM))
    return (y_p, y_s, meta_b(k_m), meta_b(v_m), heads(kwin_p), heads(vwin_p), conva_p[None], ffn_p[None],
            heads(kwin_s), heads(vwin_s), conva_s[None], ffn_s[None])
```

```python
import functools
import math

import numpy as np
import jax
import jax.numpy as jnp
from jax import lax
from jax.experimental import pallas as pl
from jax.experimental.pallas import tpu as pltpu

F32 = jnp.float32
BF16 = jnp.bfloat16

D_MODEL = 1024
CHUNK = 64
N_META = 16
D_CONV = 512
CONV_WIDTH = 31
N_HEADS = 8
N_KV_HEADS = 2
HEAD_DIM = 64
HEADS_PER_KV = N_HEADS // N_KV_HEADS
D_ATTN = N_HEADS * HEAD_DIM
D_KV = N_KV_HEADS * HEAD_DIM
WINDOW = 128
PAST_LEN = 4096
N_BUCKETS = 32
MAX_DISTANCE = 256
D_FF = 2816
FFN_CONV_WIDTH = 3
D_IN = 2 * D_CONV + D_ATTN + 2 * D_KV
EPS = 1e-6
NEG_INF = -1e30
SCALE = HEAD_DIM ** -0.5

_Q0 = 2 * D_CONV
_K0 = _Q0 + D_ATTN
_V0 = _K0 + D_KV

NKEY = 256
BAND = WINDOW + CHUNK
HALO = 32
FFN_CW = 256
TM = 512
SB = 16
VMEM_LIMIT = 60 * 1024 * 1024


def _np_bucket(rel):
    nb = N_BUCKETS // 2
    max_exact = nb // 2
    ret = np.where(rel > 0, nb, 0)
    n = np.abs(rel)
    nf = np.maximum(n, 1).astype(np.float32)
    large = max_exact + (np.log(nf / np.float32(max_exact)) / np.float32(math.log(MAX_DISTANCE / max_exact))
                         * np.float32(nb - max_exact)).astype(np.int32)
    large = np.minimum(large, nb - 1)
    return (ret + np.where(n < max_exact, n, large)).astype(np.int32)


def _bucket_tables(past_len, dec_seq):
    i = np.arange(CHUNK)[:, None]
    j = np.arange(BAND)[None, :]
    m = np.arange(N_META)[None, :]
    band = _np_bucket(j - WINDOW - i)
    pad = np.full((CHUNK, NKEY - BAND - N_META), -1, np.int32)

    def meta_for(c):
        return _np_bucket(m - (N_META + CHUNK * c + i))

    prompt = []
    for c in range(4):
        b = np.where(c - 2 + j // CHUNK >= 0, band, -1)
        prompt.append(np.concatenate([b, meta_for(c), pad], axis=1))
    assert np.array_equal(meta_for(3), meta_for(4096))
    prompt = np.stack(prompt).astype(np.int32)

    i_s = np.arange(dec_seq)[:, None]
    band_s = np.where(j < WINDOW + dec_seq, _np_bucket(j - WINDOW - i_s), -1)
    meta_s = _np_bucket(m - (N_META + past_len + i_s))
    sample = np.concatenate([band_s, meta_s, pad[:dec_seq]], axis=1)[None].astype(np.int32)

    i_m = np.arange(N_META)[:, None]
    meta_self = np.concatenate([np.full((N_META, BAND), -1, np.int32), _np_bucket(m - i_m), pad[:N_META]],
                               axis=1)[None].astype(np.int32)
    return prompt, sample, meta_self


def _bias_kernel(table_ref, bp_ref, bs_ref, bm_ref, op_ref, os_ref, om_ref):
    for b_ref, o_ref in ((bp_ref, op_ref), (bs_ref, os_ref), (bm_ref, om_ref)):
        nv, r, _ = b_ref.shape
        for v in range(nv):
            bk = b_ref[v]
            for g in range(N_KV_HEADS):
                for hl in range(HEADS_PER_KV):
                    acc = jnp.full((r, NKEY), NEG_INF, F32)
                    for b in range(N_BUCKETS):
                        acc = jnp.where(bk == b, table_ref[b, HEADS_PER_KV * g + hl], acc)
                    o_ref[v, g, r * hl:r * (hl + 1), :] = acc


def _build_bias(table, bp, bs, bm):
    shapes = [jax.ShapeDtypeStruct((b.shape[0], N_KV_HEADS, HEADS_PER_KV * b.shape[1], NKEY), F32)
              for b in (bp, bs, bm)]
    vm = pl.BlockSpec(memory_space=pltpu.VMEM)
    return pl.pallas_call(
        _bias_kernel,
        out_shape=shapes,
        in_specs=[pl.BlockSpec(memory_space=pltpu.SMEM), vm, vm, vm],
        out_specs=[vm, vm, vm],
        name="rel_bias",
    )(table, bp, bs, bm)


def _dot(a, b):
    return jnp.dot(a, b, preferred_element_type=F32)


def _dot_nt(a, b):
    return lax.dot_general(a, b, (((1,), (1,)), ((), ())), preferred_element_type=F32)


def _rms(x, g):
    ms = jnp.mean(x * x, axis=-1, keepdims=True)
    return (x * lax.rsqrt(ms + EPS)) * g


def _zero_from(x):
    u = lax.bitcast_convert_type(x, jnp.int32)
    u = lax.shift_right_logical(lax.shift_right_logical(u, jnp.int32(16)), jnp.int32(16))
    return u.astype(F32)


def _swap_halves(x):
    return pltpu.roll(x, HEAD_DIM, axis=1)


def _attention_items(nblk, r, get_q, get_k, get_v, get_bias, sinks_ref, put_out):
    def block(b):
        q = get_q(b) * SCALE
        kb = get_k(b)
        vb = get_v(b)
        lo = lax.broadcasted_iota(jnp.int32, (r, 2 * HEAD_DIM), 1) < HEAD_DIM
        for g in range(N_KV_HEADS):
            keep = lo if g == 0 else jnp.logical_not(lo)
            parts = []
            for hl in range(HEADS_PER_KV):
                h = HEADS_PER_KV * g + hl
                qh = q[:, 128 * (h // 2):128 * (h // 2) + 128]
                if h % 2 != g:
                    qh = _swap_halves(qh)
                parts.append(jnp.where(keep, qh, 0.0))
            qs = jnp.concatenate(parts, axis=0).astype(BF16)
            s = _dot_nt(qs, kb) + get_bias(b, g)
            probs = []
            for hl in range(HEADS_PER_KV):
                sink = sinks_ref[HEADS_PER_KV * g + hl]
                seg = s[r * hl:r * (hl + 1), :]
                mx = jnp.maximum(jnp.max(seg, axis=-1, keepdims=True), sink)
                e = jnp.exp(seg - mx)
                den = jnp.sum(e, axis=-1, keepdims=True) + jnp.exp(sink - mx)
                probs.append((e * (1.0 / den)).astype(BF16))
            o = _dot(jnp.concatenate(probs, axis=0), vb)
            for pp in range(2):
                even = o[r * 2 * pp:r * (2 * pp + 1)]
                odd = o[r * (2 * pp + 1):r * (2 * pp + 2)]
                if g == 1:
                    even = _swap_halves(even)
                else:
                    odd = _swap_halves(odd)
                put_out(b, g, pp, jnp.where(lo, even, odd))

    return [functools.partial(block, b) for b in range(nblk)]


def _put_heads(mix_ref, row0, r):
    def put(b, g, pp, o):
        base = D_CONV + 256 * g + 128 * pp
        mix_ref[row0(b):row0(b) + r, base:base + 128] = o.astype(BF16)
    return put


def _in_proj(x_ref, gmix_ref, w_in_ref, hb_ref, z_ref, ns, l, rc):
    g = gmix_ref[...]
    for s in range(ns):
        for r0 in range(0, l, rc):
            hb_ref[s * l + r0:s * l + r0 + rc, :] = _rms(x_ref[s, r0:r0 + rc, :], g).astype(BF16)
    z_ref[:, 0:_Q0] = _dot(hb_ref[...], w_in_ref[:, 0:_Q0])
    z_ref[:, _Q0:D_IN] = _dot(hb_ref[...], w_in_ref[:, _Q0:D_IN])


def _delay(x):
    return pltpu.roll(x, 1, axis=0)


def _conv31_block(glu_buf, u0, rc, cw_ref, l0, n):
    nb = (rc + 8) // 8
    acc = None
    for r in range(7, -1, -1):
        s = None
        for a in range(4):
            d = 8 * a + r
            if d >= CONV_WIDTH:
                continue
            blk = glu_buf[u0 - 8 - 8 * a:u0 + rc - 8 * a, l0:l0 + n].reshape(nb, 8, n)
            term = blk * cw_ref[d, :, l0:l0 + n][None]
            s = term if s is None else s + term
        s = s.reshape(rc + 8, n)
        acc = s if acc is None else s + _delay(acc)
    return acc[8:]


def _run(items):
    for item in items:
        item()


def _interleave(a, b):
    out, ia, ib = [], 0, 0
    while ia < len(a) or ib < len(b):
        if ib >= len(b) or (ia < len(a) and ia * len(b) <= ib * len(a)):
            out.append(a[ia])
            ia += 1
        else:
            out.append(b[ib])
            ib += 1
    return out


def _conv_items(z_ref, row0, l, glu_buf, cw_ref, dw_b_ref, ln_g_ref, ln_b_ref, mix_ref, rc):
    rn = min(rc, 64)

    def glu(r0):
        a = z_ref[row0 + r0:row0 + r0 + rc, 0:D_CONV]
        b = z_ref[row0 + r0:row0 + r0 + rc, D_CONV:2 * D_CONV]
        glu_buf[HALO + r0:HALO + r0 + rc, :] = a * jax.nn.sigmoid(b)

    def conv(r0):
        for l0 in range(0, D_CONV, 128):
            z_ref[row0 + r0:row0 + r0 + rc, l0:l0 + 128] = _conv31_block(glu_buf, HALO + r0, rc, cw_ref, l0, 128)

    def norm(r0):
        for q0 in range(r0, r0 + rc, rn):
            acc = z_ref[row0 + q0:row0 + q0 + rn, 0:D_CONV] + dw_b_ref[...]
            mu = jnp.mean(acc, axis=-1, keepdims=True)
            d = acc - mu
            var = jnp.mean(d * d, axis=-1, keepdims=True)
            yn = (d * lax.rsqrt(var + EPS)) * ln_g_ref[...] + ln_b_ref[...]
            mix_ref[row0 + q0:row0 + q0 + rn, 0:D_CONV] = (yn * jax.nn.sigmoid(yn)).astype(BF16)

    items = []
    for r0 in range(0, l, rc):
        items += [functools.partial(glu, r0), functools.partial(conv, r0), functools.partial(norm, r0)]
    return items


def _seq_pieces(m0, mc, l):
    pc = min(l, mc)
    return [(p0,) + divmod(m0 + p0, l) + (pc,) for p0 in range(0, mc, pc)]


def _out_proj_items(x_ref, mix_ref, w_out_ref, gffn_ref, x1_ref, hb_ref, ns, l, mc):
    def chunk(m0):
        acc = _dot(mix_ref[m0:m0 + mc, :], w_out_ref[...])
        for p0, s, r0, pc in _seq_pieces(m0, mc, l):
            x1 = x_ref[s, r0:r0 + pc, :] + acc[p0:p0 + pc]
            x1_ref[m0 + p0:m0 + p0 + pc, :] = x1
            hb_ref[m0 + p0:m0 + p0 + pc, :] = _rms(x1, gffn_ref[...]).astype(BF16)

    return [functools.partial(chunk, m0) for m0 in range(0, ns * l, mc)]


def _ffn_items(hb_ref, w_up_ref, fw_ref, fb_ref, gbuf, ubuf, y_ref, put_halo, save_carry, ns, l, rc):
    nch = D_FF // FFN_CW
    nb = (rc + 8) // 8

    def matmuls(ci):
        c = ci * FFN_CW
        gb = gbuf.at[ci % 2]
        ug = _dot(hb_ref[...], w_up_ref[:, c:c + FFN_CW])
        for s in range(ns):
            put_halo(gb, s, slice(c, c + FFN_CW))
            gb[s, 8:8 + l, :] = ug[s * l:(s + 1) * l]
        ubuf[ci % 2] = _dot(hb_ref[...], w_up_ref[:, D_FF + c:D_FF + c + FFN_CW])

    def elementwise(ci):
        c = ci * FFN_CW
        gb = gbuf.at[ci % 2]
        ub = ubuf.at[ci % 2]
        cols = slice(c, c + FFN_CW)
        w0 = fw_ref[0, :, cols][None]
        w1 = fw_ref[1, :, cols][None]
        w2 = fw_ref[2, :, cols][None]
        bb = fb_ref[:, cols][None]
        for s in range(ns):
            for r0 in range(0, l, rc):
                g = gb[s, r0:r0 + rc + 8, :].reshape(nb, 8, FFN_CW)
                t = (g * w0).reshape(rc + 8, FFN_CW)
                t = (g * w1).reshape(rc + 8, FFN_CW) + _delay(t)
                t = (g * w2 + bb).reshape(rc + 8, FFN_CW) + _delay(t)
                cv = t[8:]
                y = (cv * jax.nn.sigmoid(cv)) * ub[s * l + r0:s * l + r0 + rc, :]
                y_ref[s * l + r0:s * l + r0 + rc, cols] = y.astype(BF16)
            save_carry(gb, s, cols)

    return ([functools.partial(matmuls, ci) for ci in range(nch)],
            [functools.partial(elementwise, ci) for ci in range(nch)])


def _ffn_pipelined(mm, el):
    items = [mm[0]]
    for ci in range(len(el)):
        if ci + 1 < len(mm):
            items.append(mm[ci + 1])
        items.append(el[ci])
    return items


def _ffn_out_items(x1_ref, y_ref, w_down_ref, gfin_ref, o_ref, ns, l, mc):
    def chunk(m0):
        acc = _dot(y_ref[m0:m0 + mc, :], w_down_ref[...])
        for p0, s, r0, pc in _seq_pieces(m0, mc, l):
            x2 = x1_ref[m0 + p0:m0 + p0 + pc, :] + acc[p0:p0 + pc]
            o_ref[s, r0:r0 + pc, :] = _rms(x2, gfin_ref[...])

    return [functools.partial(chunk, m0) for m0 in range(0, ns * l, mc)]


def _meta_kernel(sinks_ref, x_ref, bias_ref, gmix_ref, w_in_ref, cw_ref, dw_b_ref, ln_g_ref, ln_b_ref,
                 w_out_ref, gffn_ref, w_upg_ref,
                 glu_o, k_o, v_o, gate_o,
                 hb_ref, z_ref, glu_buf, mix_ref):
    l = N_META
    _in_proj(x_ref, gmix_ref, w_in_ref, hb_ref, z_ref, 1, l, l)
    glu_buf[0:HALO, :] = jnp.zeros((HALO, D_CONV), F32)
    _run(_conv_items(z_ref, 0, l, glu_buf, cw_ref, dw_b_ref, ln_g_ref, ln_b_ref, mix_ref, l))
    glu_o[...] = glu_buf[HALO:HALO + l, :]
    k = z_ref[:, _K0:_K0 + D_KV]
    v = z_ref[:, _V0:_V0 + D_KV]
    k_o[...] = k
    v_o[...] = v
    zb = jnp.zeros((BAND, D_KV), BF16)
    zp = jnp.zeros((NKEY - BAND - N_META, D_KV), BF16)
    kb = jnp.concatenate([zb, k.astype(BF16), zp], axis=0)
    vb = jnp.concatenate([zb, v.astype(BF16), zp], axis=0)
    _run(_attention_items(1, l, lambda b: z_ref[:, _Q0:_Q0 + D_ATTN], lambda b: kb, lambda b: vb,
                          lambda b, g: bias_ref[0, g], sinks_ref, _put_heads(mix_ref, lambda b: 0, l)))
    x1 = x_ref[0] + _dot(mix_ref[...], w_out_ref[...])
    h2 = _rms(x1, gffn_ref[...]).astype(BF16)
    gate = _dot(h2, w_upg_ref[...])
    gate_o[...] = gate[N_META - 8:N_META, :]


def _run_meta(sinks, meta_tokens, bias_m, gmix, w_in, cw, dw_b, ln_g, ln_b, w_out, gffn, w_up):
    l = N_META
    full = lambda a: pl.BlockSpec(a.shape, lambda i: (0,) * a.ndim)
    x = meta_tokens[None]
    ins = [x, bias_m, gmix, w_in, cw, dw_b, ln_g, ln_b, w_out, gffn]
    in_specs = [pl.BlockSpec(memory_space=pltpu.SMEM)] + [full(a) for a in ins]
    in_specs.append(pl.BlockSpec((D_MODEL, D_FF), lambda i: (0, 0)))
    out_shape = [jax.ShapeDtypeStruct((l, D_CONV), F32), jax.ShapeDtypeStruct((l, D_KV), F32),
                 jax.ShapeDtypeStruct((l, D_KV), F32), jax.ShapeDtypeStruct((8, D_FF), F32)]
    return pl.pallas_call(
        _meta_kernel,
        grid=(1,),
        out_shape=out_shape,
        in_specs=in_specs,
        out_specs=[pl.BlockSpec(s.shape, lambda i: (0, 0)) for s in out_shape],
        scratch_shapes=[pltpu.VMEM((l, D_MODEL), BF16), pltpu.VMEM((l, D_IN), F32),
                        pltpu.VMEM((HALO + l, D_CONV), F32), pltpu.VMEM((l, D_MODEL), BF16)],
        compiler_params=pltpu.CompilerParams(vmem_limit_bytes=VMEM_LIMIT),
        name="meta_tokens",
    )(sinks, *ins, w_up)


def _prompt_kernel(sinks_ref, x_ref, glu_m_ref, k_m_ref, v_m_ref, gate_m_ref, bias_ref,
                   gmix_ref, w_in_ref, cw_ref, dw_b_ref, ln_g_ref, ln_b_ref, w_out_ref, gffn_ref,
                   w_up_ref, fw_ref, fb_ref, w_down_ref, gfin_ref,
                   y_o, kwin_o, vwin_o, conva_o, ffn_o,
                   hb_ref, z_ref, glu_buf, kv_buf, mix_ref, x1_ref, hb2_ref, gbuf, ubuf, yh_ref, gcarry,
                   *, nt, nsteps):
    i = pl.program_id(0)
    l = TM
    rc = 128
    nrc = l // rc
    t = lax.rem(jnp.minimum(i, nsteps - 1), nt)
    t_ffn = lax.rem(jnp.maximum(i - 1, 0), nt)
    slot = lax.rem(i, 2)
    x1_w, hb2_w = x1_ref.at[slot], hb2_ref.at[slot]
    x1_r, hb2_r = x1_ref.at[1 - slot], hb2_ref.at[1 - slot]

    @pl.when(i == 0)
    def _():
        x1_ref[1] = jnp.zeros((l, D_MODEL), F32)
        hb2_ref[1] = jnp.zeros((l, D_MODEL), BF16)

    @pl.when(t == 0)
    def _():
        glu_buf[0:HALO - N_META, :] = jnp.zeros((HALO - N_META, D_CONV), F32)
        glu_buf[HALO - N_META:HALO, :] = glu_m_ref[...]
        kv_buf[:, 0:WINDOW, :] = jnp.zeros((2, WINDOW, D_KV), BF16)

    @pl.when(t_ffn == 0)
    def _():
        gcarry[...] = gate_m_ref[...]

    def conv_done(j):
        return _zero_from(z_ref[j * rc:j * rc + 8, 0:128])

    def in_rows(j, behind=None):
        r0 = j * rc
        g = gmix_ref[...]
        if behind is not None:
            g = g + jnp.tile(conv_done(behind)[0:1, :], (1, D_MODEL // 128))
        hb_ref[r0:r0 + rc, :] = _rms(x_ref[0, r0:r0 + rc, :], g).astype(BF16)
        z_ref[r0:r0 + rc, 0:_Q0] = _dot(hb_ref[r0:r0 + rc, :], w_in_ref[:, 0:_Q0])
        z_ref[r0:r0 + rc, _Q0:D_IN] = _dot(hb_ref[r0:r0 + rc, :], w_in_ref[:, _Q0:D_IN])
        for idx, c0 in enumerate((_K0, _V0)):
            kv_buf[idx, WINDOW + r0:WINDOW + r0 + rc, :] = z_ref[r0:r0 + rc, c0:c0 + D_KV].astype(BF16)

    zp = jnp.zeros((NKEY - BAND - N_META, D_KV), BF16)
    km = jnp.concatenate([k_m_ref[...].astype(BF16), zp], axis=0)
    vm = jnp.concatenate([v_m_ref[...].astype(BF16), zp], axis=0)

    def variant(c):
        return jnp.where(t == 0, c, 3) if c < 3 else 3

    rows = lambda c: c * CHUNK
    cpr = rc // CHUNK

    def get_q(c):
        q = z_ref[rows(c):rows(c) + CHUNK, _Q0:_Q0 + D_ATTN]
        return q + jnp.tile(conv_done(min(c // cpr, nrc - 2)), (CHUNK // 8, D_ATTN // 128))

    attn_items = _attention_items(
        l // CHUNK, CHUNK, get_q,
        lambda c: jnp.concatenate([kv_buf[0, rows(c):rows(c) + BAND, :], km], axis=0),
        lambda c: jnp.concatenate([kv_buf[1, rows(c):rows(c) + BAND, :], vm], axis=0),
        lambda c, g: bias_ref[variant(c), g],
        sinks_ref, _put_heads(mix_ref, rows, CHUNK))

    conv_items = _conv_items(z_ref, 0, l, glu_buf, cw_ref, dw_b_ref, ln_g_ref, ln_b_ref, mix_ref, rc)
    per = len(conv_items) // nrc
    mixer = [functools.partial(in_rows, 0), functools.partial(in_rows, 1)]
    for j in range(nrc):
        if j >= 1 and j + 1 < nrc:
            mixer.append(functools.partial(in_rows, j + 1, behind=j - 1))
        if j >= 1:
            mixer += attn_items[cpr * (j - 1):cpr * j if j + 1 < nrc else l // CHUNK]
        mixer += conv_items[per * j:per * (j + 1)]

    def carry_state():
        glu_buf[0:HALO, :] = glu_buf[l:l + HALO, :]
        kv_buf[:, 0:WINDOW, :] = kv_buf[:, l:l + WINDOW, :]

    mixer.append(carry_state)
    mixer += _out_proj_items(x_ref, mix_ref, w_out_ref, gffn_ref, x1_w, hb2_w, 1, l, 256)

    def put_halo(gb, s, cols):
        gb[s, 0:8, :] = gcarry[:, cols]

    def save_carry(gb, s, cols):
        gcarry[:, cols] = gb[s, l:l + 8, :]

    def hold_ffn_behind(j):
        tile = hb2_r[0:16, 0:128].astype(F32) + jnp.tile(conv_done(j), (2, 1))
        hb2_r[0:16, 0:128] = tile.astype(BF16)

    mm, el = _ffn_items(hb2_r, w_up_ref, fw_ref, fb_ref, gbuf, ubuf, yh_ref, put_halo, save_carry, 1, l, rc)
    ffn = []
    for item in _ffn_pipelined(mm, el):
        for j in range(nrc - 1):
            if item is mm[(len(mm) * (j + 1)) // nrc]:
                ffn.append(functools.partial(hold_ffn_behind, j))
        ffn.append(item)
    ffn += _ffn_out_items(x1_r, yh_ref, w_down_ref, gfin_ref, y_o, 1, l, 256)

    _run(_interleave(mixer, ffn))
    ffn_o[0] = gcarry[8 - (FFN_CONV_WIDTH - 1):8, :]

    @pl.when(i < nsteps)
    def _():
        for win_o, c0 in ((kwin_o, _K0), (vwin_o, _V0)):
            win_o[0] = z_ref[l - WINDOW:l, c0:c0 + D_KV]
        conva_o[0] = glu_buf[HALO + l - (CONV_WIDTH - 1):HALO + l, :]


def _const_spec(a):
    return pl.BlockSpec(a.shape, lambda *_: (0,) * a.ndim, pipeline_mode=pl.Buffered(1))


def _run_prompt(sinks, x, glu_m, k_m, v_m, gate_m, bias_p, weights):
    bsz, seq, _ = x.shape
    assert seq % TM == 0 and TM % CHUNK == 0 and TM >= WINDOW
    nt = seq // TM
    nsteps = bsz * nt
    consts = [glu_m, k_m, v_m, gate_m, bias_p, *weights]

    def mix_tile(i):
        return jnp.minimum(i, nsteps - 1)

    def ffn_tile(i):
        return jnp.maximum(i - 1, 0)

    in_specs = ([pl.BlockSpec(memory_space=pltpu.SMEM),
                 pl.BlockSpec((1, TM, D_MODEL), lambda i: (mix_tile(i) // nt, mix_tile(i) % nt, 0))]
                + [_const_spec(a) for a in consts])
    out_shape = [jax.ShapeDtypeStruct((bsz, seq, D_MODEL), F32),
                 jax.ShapeDtypeStruct((bsz, WINDOW, D_KV), F32), jax.ShapeDtypeStruct((bsz, WINDOW, D_KV), F32),
                 jax.ShapeDtypeStruct((bsz, CONV_WIDTH - 1, D_CONV), F32),
                 jax.ShapeDtypeStruct((bsz, FFN_CONV_WIDTH - 1, D_FF), F32)]
    mixer_state = lambda s: pl.BlockSpec((1,) + s.shape[1:], lambda i: (mix_tile(i) // nt, 0, 0))
    out_specs = [pl.BlockSpec((1, TM, D_MODEL), lambda i: (ffn_tile(i) // nt, ffn_tile(i) % nt, 0)),
                 mixer_state(out_shape[1]), mixer_state(out_shape[2]), mixer_state(out_shape[3]),
                 pl.BlockSpec((1,) + out_shape[4].shape[1:], lambda i: (ffn_tile(i) // nt, 0, 0))]
    scratch = [pltpu.VMEM((TM, D_MODEL), BF16), pltpu.VMEM((TM, D_IN), F32),
               pltpu.VMEM((HALO + TM, D_CONV), F32), pltpu.VMEM((2, WINDOW + TM, D_KV), BF16),
               pltpu.VMEM((TM, D_MODEL), BF16), pltpu.VMEM((2, TM, D_MODEL), F32), pltpu.VMEM((2, TM, D_MODEL), BF16),
               pltpu.VMEM((2, 1, 8 + TM, FFN_CW), F32), pltpu.VMEM((2, TM, FFN_CW), F32),
               pltpu.VMEM((TM, D_FF), BF16), pltpu.VMEM((8, D_FF), F32)]
    return pl.pallas_call(
        functools.partial(_prompt_kernel, nt=nt, nsteps=nsteps),
        grid=(nsteps + 1,),
        out_shape=out_shape,
        in_specs=in_specs,
        out_specs=out_specs,
        scratch_shapes=scratch,
        compiler_params=pltpu.CompilerParams(dimension_semantics=("arbitrary",), vmem_limit_bytes=VMEM_LIMIT),
        name="prompt_step",
    )(sinks, x, *consts)


def _sample_kernel(sinks_ref, x_ref, kmeta_ref, vmeta_ref, kwin_ref, vwin_ref, conva_ref, ffn_ref, bias_ref,
                   gmix_ref, w_in_ref, cw_ref, dw_b_ref, ln_g_ref, ln_b_ref, w_out_ref, gffn_ref,
                   w_up_ref, fw_ref, fb_ref, w_down_ref, gfin_ref,
                   y_o, kwin_o, vwin_o, conva_o, ffn_o,
                   hb_ref, z_ref, glu_buf, mix_ref, x1_ref, gbuf, ubuf, yh_ref):
    ns, l, _ = x_ref.shape
    _in_proj(x_ref, gmix_ref, w_in_ref, hb_ref, z_ref, ns, l, l)
    glu_buf[0:HALO - 30, :] = jnp.zeros((HALO - 30, D_CONV), F32)
    zgap = jnp.zeros((BAND - WINDOW - l, D_KV), BF16)
    zp = jnp.zeros((NKEY - BAND - N_META, D_KV), BF16)
    def load_state(s):
        glu_buf[HALO - (CONV_WIDTH - 1):HALO, :] = conva_ref[s]

    def save_state(s):
        conva_o[s] = glu_buf[HALO + l - (CONV_WIDTH - 1):HALO + l, :]
        for win_ref, win_o, c0 in ((kwin_ref, kwin_o, _K0), (vwin_ref, vwin_o, _V0)):
            win_o[s, 0:WINDOW - l, :] = win_ref[s, l:WINDOW, :]
            win_o[s, WINDOW - l:WINDOW, :] = z_ref[s * l:(s + 1) * l, c0:c0 + D_KV]

    conv_items = []
    for s in range(ns):
        conv_items += ([functools.partial(load_state, s)]
                       + _conv_items(z_ref, s * l, l, glu_buf, cw_ref, dw_b_ref, ln_g_ref, ln_b_ref, mix_ref, l)
                       + [functools.partial(save_state, s)])

    def key_slots(win_ref, meta_ref, c0):
        def get(s):
            new = z_ref[s * l:(s + 1) * l, c0:c0 + D_KV]
            return jnp.concatenate([win_ref[s].astype(BF16), new.astype(BF16), zgap,
                                    meta_ref[s].astype(BF16), zp], axis=0)
        return get

    rows = lambda s: s * l
    attn_items = _attention_items(ns, l, lambda s: z_ref[rows(s):rows(s) + l, _Q0:_Q0 + D_ATTN],
                                  key_slots(kwin_ref, kmeta_ref, _K0), key_slots(vwin_ref, vmeta_ref, _V0),
                                  lambda s, g: bias_ref[0, g], sinks_ref, _put_heads(mix_ref, rows, l))
    _run(_interleave(conv_items, attn_items))

    _run(_out_proj_items(x_ref, mix_ref, w_out_ref, gffn_ref, x1_ref, hb_ref, ns, l, 256))

    nprev = FFN_CONV_WIDTH - 1
    for i in range(2):
        for s in range(ns):
            gbuf[i, s, 0:8, :] = jnp.zeros((8, FFN_CW), F32)

    def put_halo(gb, s, cols):
        gb[s, 8 - nprev:8, :] = ffn_ref[s, :, cols]

    def save_carry(gb, s, cols):
        ffn_o[s, :, cols] = gb[s, 8 + l - nprev:8 + l, :]

    _run(_ffn_pipelined(*_ffn_items(hb_ref, w_up_ref, fw_ref, fb_ref, gbuf, ubuf, yh_ref, put_halo, save_carry,
                                    ns, l, l)))
    _run(_ffn_out_items(x1_ref, yh_ref, w_down_ref, gfin_ref, y_o, ns, l, 256))


def _run_sample(sinks, x, kmeta, vmeta, kwin, vwin, conva, ffn, bias_s, weights):
    nstream, l, _ = x.shape
    assert nstream % SB == 0 and l % 16 == 0 and l <= CHUNK and kwin.shape[1] == WINDOW
    rows = SB * l
    per_stream = [x, kmeta, vmeta, kwin, vwin, conva, ffn]
    consts = [bias_s, *weights]
    in_specs = ([pl.BlockSpec(memory_space=pltpu.SMEM)]
                + [pl.BlockSpec((SB,) + a.shape[1:], lambda i: (i, 0, 0)) for a in per_stream]
                + [_const_spec(a) for a in consts])
    out_shape = [jax.ShapeDtypeStruct(a.shape, F32) for a in (x, kwin, vwin, conva, ffn)]
    out_specs = [pl.BlockSpec((SB,) + s.shape[1:], lambda i: (i, 0, 0)) for s in out_shape]
    scratch = [pltpu.VMEM((rows, D_MODEL), BF16), pltpu.VMEM((rows, D_IN), F32),
               pltpu.VMEM((HALO + l, D_CONV), F32),
               pltpu.VMEM((rows, D_MODEL), BF16), pltpu.VMEM((rows, D_MODEL), F32),
               pltpu.VMEM((2, SB, 8 + l, FFN_CW), F32), pltpu.VMEM((2, rows, FFN_CW), F32),
               pltpu.VMEM((rows, D_FF), BF16)]
    return pl.pallas_call(
        _sample_kernel,
        grid=(nstream // SB,),
        out_shape=out_shape,
        in_specs=in_specs,
        out_specs=out_specs,
        scratch_shapes=scratch,
        compiler_params=pltpu.CompilerParams(dimension_semantics=("arbitrary",), vmem_limit_bytes=VMEM_LIMIT),
        name="sample_step",
    )(sinks, *per_stream, *consts)


def kernel(x_prompt, x_sample, cache_k_meta, cache_v_meta, cache_k_win, cache_v_win, state_conv_a, state_conv_ffn,
           meta_tokens, rel_bias_table, norm_mix, w_in, conv_dw_w, conv_dw_b, conv_ln_g, conv_ln_b, attn_sinks,
           w_out, norm_ffn, w_up, ffn_dw_w, ffn_dw_b, w_down, norm_final):
    depth = w_in.shape[0]
    assert depth == 1, "single-layer step"
    bsz = x_prompt.shape[0]
    nstream, dec_seq, _ = x_sample.shape
    win = cache_k_win.shape[2]
    assert win == WINDOW

    bp, bs, bm = _bucket_tables(PAST_LEN, dec_seq)
    bias_p, bias_s, bias_m = _build_bias(rel_bias_table, jnp.asarray(bp), jnp.asarray(bs), jnp.asarray(bm))

    row = lambda a: a.reshape(1, -1)
    w_in_b = w_in[0].astype(BF16)
    w_out_b = w_out[0].astype(BF16)
    w_up_b = w_up[0].astype(BF16)
    w_down_b = w_down[0].astype(BF16)
    sinks = attn_sinks[0]
    gmix, gffn, gfin = row(norm_mix[0]), row(norm_ffn[0]), row(norm_final)
    cw = jnp.broadcast_to(conv_dw_w[0][::-1][:, None, :], (CONV_WIDTH, 8, D_CONV))
    dw_b = row(conv_dw_b[0])
    ln_g, ln_b = row(conv_ln_g[0]), row(conv_ln_b[0])
    fw = jnp.broadcast_to(ffn_dw_w[0][:, None, :], (FFN_CONV_WIDTH, 8, D_FF))
    fb = jnp.broadcast_to(ffn_dw_b[0][None, :], (8, D_FF))

    glu_m, k_m, v_m, gate_m = _run_meta(sinks, meta_tokens, bias_m, gmix, w_in_b, cw, dw_b, ln_g, ln_b,
                                        w_out_b, gffn, w_up_b)

    weights = [gmix, w_in_b, cw, dw_b, ln_g, ln_b, w_out_b, gffn, w_up_b, fw, fb, w_down_b, gfin]
    y_p, kwin_p, vwin_p, conva_p, ffn_p = _run_prompt(sinks, x_prompt, glu_m, k_m, v_m, gate_m, bias_p, weights)

    kv = lambda a: a.reshape(a.shape[0], a.shape[1], D_KV)
    y_s, kwin_s, vwin_s, conva_s, ffn_s = _run_sample(
        sinks, x_sample, kv(cache_k_meta[0]), kv(cache_v_meta[0]), kv(cache_k_win[0]), kv(cache_v_win[0]),
        state_conv_a[0], state_conv_ffn[0], bias_s, weights)

    heads = lambda a: a.reshape(1, a.shape[0], a.shape[1], N_KV_HEADS, HEAD_DIM)
    meta_b = lambda a: jnp.broadcast_to(a.reshape(1, 1, N_META, N_KV_HEADS, HEAD_DIM),
                                        (1, bsz, N_META, N_KV_HEADS, HEAD_DIM))
    return (y_p, y_s, meta_b(k_m), meta_b(v_m), heads(kwin_p), heads(vwin_p), conva_p[None], ffn_p[None],
            heads(kwin_s), heads(vwin_s), conva_s[None], ffn_s[None])
```

```python
import functools
import math

import numpy as np
import jax
import jax.numpy as jnp
from jax import lax
from jax.experimental import pallas as pl
from jax.experimental.pallas import tpu as pltpu

F32 = jnp.float32
BF16 = jnp.bfloat16

D_MODEL = 1024
CHUNK = 64
N_META = 16
D_CONV = 512
CONV_WIDTH = 31
N_HEADS = 8
N_KV_HEADS = 2
HEAD_DIM = 64
HEADS_PER_KV = N_HEADS // N_KV_HEADS
D_ATTN = N_HEADS * HEAD_DIM
D_KV = N_KV_HEADS * HEAD_DIM
WINDOW = 128
PAST_LEN = 4096
N_BUCKETS = 32
MAX_DISTANCE = 256
D_FF = 2816
FFN_CONV_WIDTH = 3
D_IN = 2 * D_CONV + D_ATTN + 2 * D_KV
EPS = 1e-6
NEG_INF = -1e30
SCALE = HEAD_DIM ** -0.5

_Q0 = 2 * D_CONV
_K0 = _Q0 + D_ATTN
_V0 = _K0 + D_KV

NKEY = 256
BAND = WINDOW + CHUNK
HALO = 32
FFN_CW = 256
TM = 512
SB = 16
VMEM_LIMIT = 60 * 1024 * 1024


def _np_bucket(rel):
    nb = N_BUCKETS // 2
    max_exact = nb // 2
    ret = np.where(rel > 0, nb, 0)
    n = np.abs(rel)
    nf = np.maximum(n, 1).astype(np.float32)
    large = max_exact + (np.log(nf / np.float32(max_exact)) / np.float32(math.log(MAX_DISTANCE / max_exact))
                         * np.float32(nb - max_exact)).astype(np.int32)
    large = np.minimum(large, nb - 1)
    return (ret + np.where(n < max_exact, n, large)).astype(np.int32)


def _bucket_tables(past_len, dec_seq):
    i = np.arange(CHUNK)[:, None]
    j = np.arange(BAND)[None, :]
    m = np.arange(N_META)[None, :]
    band = _np_bucket(j - WINDOW - i)
    pad = np.full((CHUNK, NKEY - BAND - N_META), -1, np.int32)

    def meta_for(c):
        return _np_bucket(m - (N_META + CHUNK * c + i))

    prompt = []
    for c in range(4):
        b = np.where(c - 2 + j // CHUNK >= 0, band, -1)
        prompt.append(np.concatenate([b, meta_for(c), pad], axis=1))
    assert np.array_equal(meta_for(3), meta_for(4096))
    prompt = np.stack(prompt).astype(np.int32)

    i_s = np.arange(dec_seq)[:, None]
    band_s = np.where(j < WINDOW + dec_seq, _np_bucket(j - WINDOW - i_s), -1)
    meta_s = _np_bucket(m - (N_META + past_len + i_s))
    sample = np.concatenate([band_s, meta_s, pad[:dec_seq]], axis=1)[None].astype(np.int32)

    i_m = np.arange(N_META)[:, None]
    meta_self = np.concatenate([np.full((N_META, BAND), -1, np.int32), _np_bucket(m - i_m), pad[:N_META]],
                               axis=1)[None].astype(np.int32)
    return prompt, sample, meta_self


def _bias_kernel(table_ref, bp_ref, bs_ref, bm_ref, op_ref, os_ref, om_ref):
    for b_ref, o_ref in ((bp_ref, op_ref), (bs_ref, os_ref), (bm_ref, om_ref)):
        nv, r, _ = b_ref.shape
        for v in range(nv):
            bk = b_ref[v]
            for h in range(N_HEADS):
                acc = jnp.full((r, NKEY), NEG_INF, F32)
                for b in range(N_BUCKETS):
                    acc = jnp.where(bk == b, table_ref[b, h], acc)
                o_ref[v, r * h:r * (h + 1), :] = acc


def _build_bias(table, bp, bs, bm):
    shapes = [jax.ShapeDtypeStruct((b.shape[0], N_HEADS * b.shape[1], NKEY), F32) for b in (bp, bs, bm)]
    vm = pl.BlockSpec(memory_space=pltpu.VMEM)
    return pl.pallas_call(
        _bias_kernel,
        out_shape=shapes,
        in_specs=[pl.BlockSpec(memory_space=pltpu.SMEM), vm, vm, vm],
        out_specs=[vm, vm, vm],
        name="rel_bias",
    )(table, bp, bs, bm)


def _dot(a, b):
    return jnp.dot(a, b, preferred_element_type=F32)


def _dot_nt(a, b):
    return lax.dot_general(a, b, (((1,), (1,)), ((), ())), preferred_element_type=F32)


def _rms(x, g):
    ms = jnp.mean(x * x, axis=-1, keepdims=True)
    return (x * lax.rsqrt(ms + EPS)) * g


def _zero_from(x):
    u = lax.bitcast_convert_type(x, jnp.int32)
    u = lax.shift_right_logical(lax.shift_right_logical(u, jnp.int32(16)), jnp.int32(16))
    return u.astype(F32)


def _swap_halves(x):
    return pltpu.roll(x, HEAD_DIM, axis=1)


def _attention_items(nblk, r, get_q, get_k, get_v, get_bias, sinks_ref, put_out):
    def block(b):
        q = get_q(b) * SCALE
        kb = get_k(b)
        vb = get_v(b)
        lo = lax.broadcasted_iota(jnp.int32, (r, 2 * HEAD_DIM), 1) < HEAD_DIM
        hi = jnp.logical_not(lo)
        parts = []
        for h in range(N_HEADS):
            g = h // HEADS_PER_KV
            qh = q[:, 128 * (h // 2):128 * (h // 2) + 128]
            if h % 2 != g:
                qh = _swap_halves(qh)
            parts.append(jnp.where(hi if g else lo, qh, 0.0))
        qs = jnp.concatenate(parts, axis=0).astype(BF16)
        s = _dot_nt(qs, kb) + get_bias(b)
        probs = []
        for h in range(N_HEADS):
            sink = sinks_ref[h]
            seg = s[r * h:r * (h + 1), :]
            mx = jnp.maximum(jnp.max(seg, axis=-1, keepdims=True), sink)
            e = jnp.exp(seg - mx)
            den = jnp.sum(e, axis=-1, keepdims=True) + jnp.exp(sink - mx)
            probs.append((e * (1.0 / den)).astype(BF16))
        o = _dot(jnp.concatenate(probs, axis=0), vb)
        for p in range(N_HEADS // 2):
            g = (2 * p) // HEADS_PER_KV
            even = o[r * 2 * p:r * (2 * p + 1)]
            odd = o[r * (2 * p + 1):r * (2 * p + 2)]
            if g == 1:
                even = _swap_halves(even)
            else:
                odd = _swap_halves(odd)
            put_out(b, p, jnp.where(lo, even, odd))

    return [functools.partial(block, b) for b in range(nblk)]


def _put_heads(mix_ref, row0, r):
    def put(b, p, o):
        base = D_CONV + 128 * p
        mix_ref[row0(b):row0(b) + r, base:base + 128] = o.astype(BF16)
    return put


def _in_proj(x_ref, gmix_ref, w_in_ref, hb_ref, z_ref, ns, l, rc):
    g = gmix_ref[...]
    for s in range(ns):
        for r0 in range(0, l, rc):
            hb_ref[s * l + r0:s * l + r0 + rc, :] = _rms(x_ref[s, r0:r0 + rc, :], g).astype(BF16)
    z_ref[:, 0:_Q0] = _dot(hb_ref[...], w_in_ref[:, 0:_Q0])
    z_ref[:, _Q0:D_IN] = _dot(hb_ref[...], w_in_ref[:, _Q0:D_IN])


def _delay(x):
    return pltpu.roll(x, 1, axis=0)


def _conv31_block(glu_buf, u0, rc, cw_ref, l0, n):
    nb = (rc + 8) // 8
    acc = None
    for r in range(7, -1, -1):
        s = None
        for a in range(4):
            d = 8 * a + r
            if d >= CONV_WIDTH:
                continue
            blk = glu_buf[u0 - 8 - 8 * a:u0 + rc - 8 * a, l0:l0 + n].reshape(nb, 8, n)
            term = blk * cw_ref[d, :, l0:l0 + n][None]
            s = term if s is None else s + term
        s = s.reshape(rc + 8, n)
        acc = s if acc is None else s + _delay(acc)
    return acc[8:]


def _run(items):
    for item in items:
        item()


def _interleave(a, b):
    out, ia, ib = [], 0, 0
    while ia < len(a) or ib < len(b):
        if ib >= len(b) or (ia < len(a) and ia * len(b) <= ib * len(a)):
            out.append(a[ia])
            ia += 1
        else:
            out.append(b[ib])
            ib += 1
    return out


def _conv_items(z_ref, row0, l, glu_buf, cw_ref, dw_b_ref, ln_g_ref, ln_b_ref, mix_ref, rc):
    rn = min(rc, 64)

    def glu(r0):
        a = z_ref[row0 + r0:row0 + r0 + rc, 0:D_CONV]
        b = z_ref[row0 + r0:row0 + r0 + rc, D_CONV:2 * D_CONV]
        glu_buf[HALO + r0:HALO + r0 + rc, :] = a * jax.nn.sigmoid(b)

    def conv(r0):
        for l0 in range(0, D_CONV, 128):
            z_ref[row0 + r0:row0 + r0 + rc, l0:l0 + 128] = _conv31_block(glu_buf, HALO + r0, rc, cw_ref, l0, 128)

    def norm(r0):
        for q0 in range(r0, r0 + rc, rn):
            acc = z_ref[row0 + q0:row0 + q0 + rn, 0:D_CONV] + dw_b_ref[...]
            mu = jnp.mean(acc, axis=-1, keepdims=True)
            d = acc - mu
            var = jnp.mean(d * d, axis=-1, keepdims=True)
            yn = (d * lax.rsqrt(var + EPS)) * ln_g_ref[...] + ln_b_ref[...]
            mix_ref[row0 + q0:row0 + q0 + rn, 0:D_CONV] = (yn * jax.nn.sigmoid(yn)).astype(BF16)

    items = []
    for r0 in range(0, l, rc):
        items += [functools.partial(glu, r0), functools.partial(conv, r0), functools.partial(norm, r0)]
    return items


def _seq_pieces(m0, mc, l):
    pc = min(l, mc)
    return [(p0,) + divmod(m0 + p0, l) + (pc,) for p0 in range(0, mc, pc)]


def _out_proj_items(x_ref, mix_ref, w_out_ref, gffn_ref, x1_ref, hb_ref, ns, l, mc):
    def chunk(m0):
        acc = _dot(mix_ref[m0:m0 + mc, :], w_out_ref[...])
        for p0, s, r0, pc in _seq_pieces(m0, mc, l):
            x1 = x_ref[s, r0:r0 + pc, :] + acc[p0:p0 + pc]
            x1_ref[m0 + p0:m0 + p0 + pc, :] = x1
            hb_ref[m0 + p0:m0 + p0 + pc, :] = _rms(x1, gffn_ref[...]).astype(BF16)

    return [functools.partial(chunk, m0) for m0 in range(0, ns * l, mc)]


def _ffn_items(hb_ref, w_up_ref, fw_ref, fb_ref, gbuf, ubuf, y_ref, put_halo, save_carry, ns, l, rc):
    nch = D_FF // FFN_CW
    nb = (rc + 8) // 8

    def matmuls(ci):
        c = ci * FFN_CW
        gb = gbuf.at[ci % 2]
        ug = _dot(hb_ref[...], w_up_ref[:, c:c + FFN_CW])
        for s in range(ns):
            put_halo(gb, s, slice(c, c + FFN_CW))
            gb[s, 8:8 + l, :] = ug[s * l:(s + 1) * l]
        ubuf[ci % 2] = _dot(hb_ref[...], w_up_ref[:, D_FF + c:D_FF + c + FFN_CW])

    def elementwise(ci):
        c = ci * FFN_CW
        gb = gbuf.at[ci % 2]
        ub = ubuf.at[ci % 2]
        cols = slice(c, c + FFN_CW)
        w0 = fw_ref[0, :, cols][None]
        w1 = fw_ref[1, :, cols][None]
        w2 = fw_ref[2, :, cols][None]
        bb = fb_ref[:, cols][None]
        for s in range(ns):
            for r0 in range(0, l, rc):
                g = gb[s, r0:r0 + rc + 8, :].reshape(nb, 8, FFN_CW)
                t = (g * w0).reshape(rc + 8, FFN_CW)
                t = (g * w1).reshape(rc + 8, FFN_CW) + _delay(t)
                t = (g * w2 + bb).reshape(rc + 8, FFN_CW) + _delay(t)
                cv = t[8:]
                y = (cv * jax.nn.sigmoid(cv)) * ub[s * l + r0:s * l + r0 + rc, :]
                y_ref[s * l + r0:s * l + r0 + rc, cols] = y.astype(BF16)
            save_carry(gb, s, cols)

    return ([functools.partial(matmuls, ci) for ci in range(nch)],
            [functools.partial(elementwise, ci) for ci in range(nch)])


def _ffn_pipelined(mm, el):
    items = [mm[0]]
    for ci in range(len(el)):
        if ci + 1 < len(mm):
            items.append(mm[ci + 1])
        items.append(el[ci])
    return items


def _ffn_out_items(x1_ref, y_ref, w_down_ref, gfin_ref, o_ref, ns, l, mc):
    def chunk(m0):
        acc = _dot(y_ref[m0:m0 + mc, :], w_down_ref[...])
        for p0, s, r0, pc in _seq_pieces(m0, mc, l):
            x2 = x1_ref[m0 + p0:m0 + p0 + pc, :] + acc[p0:p0 + pc]
            o_ref[s, r0:r0 + pc, :] = _rms(x2, gfin_ref[...])

    return [functools.partial(chunk, m0) for m0 in range(0, ns * l, mc)]


def _meta_kernel(sinks_ref, x_ref, bias_ref, gmix_ref, w_in_ref, cw_ref, dw_b_ref, ln_g_ref, ln_b_ref,
                 w_out_ref, gffn_ref, w_upg_ref,
                 glu_o, k_o, v_o, gate_o,
                 hb_ref, z_ref, glu_buf, mix_ref):
    l = N_META
    _in_proj(x_ref, gmix_ref, w_in_ref, hb_ref, z_ref, 1, l, l)
    glu_buf[0:HALO, :] = jnp.zeros((HALO, D_CONV), F32)
    _run(_conv_items(z_ref, 0, l, glu_buf, cw_ref, dw_b_ref, ln_g_ref, ln_b_ref, mix_ref, l))
    glu_o[...] = glu_buf[HALO:HALO + l, :]
    k = z_ref[:, _K0:_K0 + D_KV]
    v = z_ref[:, _V0:_V0 + D_KV]
    k_o[...] = k
    v_o[...] = v
    zb = jnp.zeros((BAND, D_KV), BF16)
    zp = jnp.zeros((NKEY - BAND - N_META, D_KV), BF16)
    kb = jnp.concatenate([zb, k.astype(BF16), zp], axis=0)
    vb = jnp.concatenate([zb, v.astype(BF16), zp], axis=0)
    _run(_attention_items(1, l, lambda b: z_ref[:, _Q0:_Q0 + D_ATTN], lambda b: kb, lambda b: vb,
                          lambda b: bias_ref[0], sinks_ref, _put_heads(mix_ref, lambda b: 0, l)))
    x1 = x_ref[0] + _dot(mix_ref[...], w_out_ref[...])
    h2 = _rms(x1, gffn_ref[...]).astype(BF16)
    gate = _dot(h2, w_upg_ref[...])
    gate_o[...] = gate[N_META - 8:N_META, :]


def _run_meta(sinks, meta_tokens, bias_m, gmix, w_in, cw, dw_b, ln_g, ln_b, w_out, gffn, w_up):
    l = N_META
    full = lambda a: pl.BlockSpec(a.shape, lambda i: (0,) * a.ndim)
    x = meta_tokens[None]
    ins = [x, bias_m, gmix, w_in, cw, dw_b, ln_g, ln_b, w_out, gffn]
    in_specs = [pl.BlockSpec(memory_space=pltpu.SMEM)] + [full(a) for a in ins]
    in_specs.append(pl.BlockSpec((D_MODEL, D_FF), lambda i: (0, 0)))
    out_shape = [jax.ShapeDtypeStruct((l, D_CONV), F32), jax.ShapeDtypeStruct((l, D_KV), F32),
                 jax.ShapeDtypeStruct((l, D_KV), F32), jax.ShapeDtypeStruct((8, D_FF), F32)]
    return pl.pallas_call(
        _meta_kernel,
        grid=(1,),
        out_shape=out_shape,
        in_specs=in_specs,
        out_specs=[pl.BlockSpec(s.shape, lambda i: (0, 0)) for s in out_shape],
        scratch_shapes=[pltpu.VMEM((l, D_MODEL), BF16), pltpu.VMEM((l, D_IN), F32),
                        pltpu.VMEM((HALO + l, D_CONV), F32), pltpu.VMEM((l, D_MODEL), BF16)],
        compiler_params=pltpu.CompilerParams(vmem_limit_bytes=VMEM_LIMIT),
        name="meta_tokens",
    )(sinks, *ins, w_up)


def _prompt_kernel(sinks_ref, x_ref, glu_m_ref, k_m_ref, v_m_ref, gate_m_ref, bias_ref,
                   gmix_ref, w_in_ref, cw_ref, dw_b_ref, ln_g_ref, ln_b_ref, w_out_ref, gffn_ref,
                   w_up_ref, fw_ref, fb_ref, w_down_ref, gfin_ref,
                   y_o, kwin_o, vwin_o, conva_o, ffn_o,
                   hb_ref, z_ref, glu_buf, kv_buf, mix_ref, x1_ref, gbuf, ubuf, yh_ref, gcarry):
    t = pl.program_id(1)
    l = TM
    rc = 128
    nrc = l // rc

    @pl.when(t == 0)
    def _():
        glu_buf[0:HALO - N_META, :] = jnp.zeros((HALO - N_META, D_CONV), F32)
        glu_buf[HALO - N_META:HALO, :] = glu_m_ref[...]
        kv_buf[:, 0:WINDOW, :] = jnp.zeros((2, WINDOW, D_KV), BF16)
        gcarry[...] = gate_m_ref[...]

    def conv_done(j):
        return _zero_from(z_ref[j * rc:j * rc + 8, 0:128])

    def in_rows(j, behind=None):
        r0 = j * rc
        g = gmix_ref[...]
        if behind is not None:
            g = g + jnp.tile(conv_done(behind)[0:1, :], (1, D_MODEL // 128))
        hb_ref[r0:r0 + rc, :] = _rms(x_ref[0, r0:r0 + rc, :], g).astype(BF16)
        z_ref[r0:r0 + rc, 0:_Q0] = _dot(hb_ref[r0:r0 + rc, :], w_in_ref[:, 0:_Q0])
        z_ref[r0:r0 + rc, _Q0:D_IN] = _dot(hb_ref[r0:r0 + rc, :], w_in_ref[:, _Q0:D_IN])
        for idx, c0 in enumerate((_K0, _V0)):
            kv_buf[idx, WINDOW + r0:WINDOW + r0 + rc, :] = z_ref[r0:r0 + rc, c0:c0 + D_KV].astype(BF16)

    zp = jnp.zeros((NKEY - BAND - N_META, D_KV), BF16)
    km = jnp.concatenate([k_m_ref[...].astype(BF16), zp], axis=0)
    vm = jnp.concatenate([v_m_ref[...].astype(BF16), zp], axis=0)

    def variant(c):
        return jnp.where(t == 0, c, 3) if c < 3 else 3

    rows = lambda c: c * CHUNK
    cpr = rc // CHUNK

    def get_q(c):
        q = z_ref[rows(c):rows(c) + CHUNK, _Q0:_Q0 + D_ATTN]
        return q + jnp.tile(conv_done(min(c // cpr, nrc - 2)), (CHUNK // 8, D_ATTN // 128))

    attn_items = _attention_items(
        l // CHUNK, CHUNK, get_q,
        lambda c: jnp.concatenate([kv_buf[0, rows(c):rows(c) + BAND, :], km], axis=0),
        lambda c: jnp.concatenate([kv_buf[1, rows(c):rows(c) + BAND, :], vm], axis=0),
        lambda c: bias_ref[variant(c)],
        sinks_ref, _put_heads(mix_ref, rows, CHUNK))

    conv_items = _conv_items(z_ref, 0, l, glu_buf, cw_ref, dw_b_ref, ln_g_ref, ln_b_ref, mix_ref, rc)
    per = len(conv_items) // nrc
    mixer = [functools.partial(in_rows, 0), functools.partial(in_rows, 1)]
    for j in range(nrc):
        if j >= 1 and j + 1 < nrc:
            mixer.append(functools.partial(in_rows, j + 1, behind=j - 1))
        if j >= 1:
            mixer += attn_items[cpr * (j - 1):cpr * j if j + 1 < nrc else l // CHUNK]
        mixer += conv_items[per * j:per * (j + 1)]

    def carry_state():
        glu_buf[0:HALO, :] = glu_buf[l:l + HALO, :]
        kv_buf[:, 0:WINDOW, :] = kv_buf[:, l:l + WINDOW, :]

    _run(mixer)
    for win_o, c0 in ((kwin_o, _K0), (vwin_o, _V0)):
        win_o[0] = z_ref[l - WINDOW:l, c0:c0 + D_KV]
    conva_o[0] = glu_buf[HALO + l - (CONV_WIDTH - 1):HALO + l, :]
    carry_state()
    _run(_out_proj_items(x_ref, mix_ref, w_out_ref, gffn_ref, x1_ref, hb_ref, 1, l, 256))

    def put_halo(gb, s, cols):
        gb[s, 0:8, :] = gcarry[:, cols]

    def save_carry(gb, s, cols):
        gcarry[:, cols] = gb[s, l:l + 8, :]

    _run(_ffn_pipelined(*_ffn_items(hb_ref, w_up_ref, fw_ref, fb_ref, gbuf, ubuf, yh_ref, put_halo, save_carry,
                                    1, l, rc)))
    ffn_o[0] = gcarry[8 - (FFN_CONV_WIDTH - 1):8, :]
    _run(_ffn_out_items(x1_ref, yh_ref, w_down_ref, gfin_ref, y_o, 1, l, 256))


def _const_spec(a):
    return pl.BlockSpec(a.shape, lambda *_: (0,) * a.ndim, pipeline_mode=pl.Buffered(1))


def _run_prompt(sinks, x, glu_m, k_m, v_m, gate_m, bias_p, weights):
    bsz, seq, _ = x.shape
    assert seq % TM == 0 and TM % CHUNK == 0 and TM >= WINDOW
    consts = [glu_m, k_m, v_m, gate_m, bias_p, *weights]
    in_specs = ([pl.BlockSpec(memory_space=pltpu.SMEM), pl.BlockSpec((1, TM, D_MODEL), lambda b, t: (b, t, 0))]
                + [_const_spec(a) for a in consts])
    out_shape = [jax.ShapeDtypeStruct((bsz, seq, D_MODEL), F32),
                 jax.ShapeDtypeStruct((bsz, WINDOW, D_KV), F32), jax.ShapeDtypeStruct((bsz, WINDOW, D_KV), F32),
                 jax.ShapeDtypeStruct((bsz, CONV_WIDTH - 1, D_CONV), F32),
                 jax.ShapeDtypeStruct((bsz, FFN_CONV_WIDTH - 1, D_FF), F32)]
    out_specs = [pl.BlockSpec((1, TM, D_MODEL), lambda b, t: (b, t, 0))] + [
        pl.BlockSpec((1,) + s.shape[1:], lambda b, t: (b, 0, 0)) for s in out_shape[1:]]
    scratch = [pltpu.VMEM((TM, D_MODEL), BF16), pltpu.VMEM((TM, D_IN), F32),
               pltpu.VMEM((HALO + TM, D_CONV), F32), pltpu.VMEM((2, WINDOW + TM, D_KV), BF16),
               pltpu.VMEM((TM, D_MODEL), BF16), pltpu.VMEM((TM, D_MODEL), F32),
               pltpu.VMEM((2, 1, 8 + TM, FFN_CW), F32), pltpu.VMEM((2, TM, FFN_CW), F32),
               pltpu.VMEM((TM, D_FF), BF16), pltpu.VMEM((8, D_FF), F32)]
    return pl.pallas_call(
        _prompt_kernel,
        grid=(bsz, seq // TM),
        out_shape=out_shape,
        in_specs=in_specs,
        out_specs=out_specs,
        scratch_shapes=scratch,
        compiler_params=pltpu.CompilerParams(dimension_semantics=("arbitrary", "arbitrary"),
                                             vmem_limit_bytes=VMEM_LIMIT),
        name="prompt_step",
    )(sinks, x, *consts)


def _sample_kernel(sinks_ref, x_ref, kmeta_ref, vmeta_ref, kwin_ref, vwin_ref, conva_ref, ffn_ref, bias_ref,
                   gmix_ref, w_in_ref, cw_ref, dw_b_ref, ln_g_ref, ln_b_ref, w_out_ref, gffn_ref,
                   w_up_ref, fw_ref, fb_ref, w_down_ref, gfin_ref,
                   y_o, kwin_o, vwin_o, conva_o, ffn_o,
                   hb_ref, z_ref, glu_buf, mix_ref, x1_ref, gbuf, ubuf, yh_ref):
    ns, l, _ = x_ref.shape
    _in_proj(x_ref, gmix_ref, w_in_ref, hb_ref, z_ref, ns, l, l)
    glu_buf[0:HALO - 30, :] = jnp.zeros((HALO - 30, D_CONV), F32)
    zgap = jnp.zeros((BAND - WINDOW - l, D_KV), BF16)
    zp = jnp.zeros((NKEY - BAND - N_META, D_KV), BF16)
    def load_state(s):
        glu_buf[HALO - (CONV_WIDTH - 1):HALO, :] = conva_ref[s]

    def save_state(s):
        conva_o[s] = glu_buf[HALO + l - (CONV_WIDTH - 1):HALO + l, :]
        for win_ref, win_o, c0 in ((kwin_ref, kwin_o, _K0), (vwin_ref, vwin_o, _V0)):
            win_o[s, 0:WINDOW - l, :] = win_ref[s, l:WINDOW, :]
            win_o[s, WINDOW - l:WINDOW, :] = z_ref[s * l:(s + 1) * l, c0:c0 + D_KV]

    conv_items = []
    for s in range(ns):
        conv_items += ([functools.partial(load_state, s)]
                       + _conv_items(z_ref, s * l, l, glu_buf, cw_ref, dw_b_ref, ln_g_ref, ln_b_ref, mix_ref, l)
                       + [functools.partial(save_state, s)])

    def key_slots(win_ref, meta_ref, c0):
        def get(s):
            new = z_ref[s * l:(s + 1) * l, c0:c0 + D_KV]
            return jnp.concatenate([win_ref[s].astype(BF16), new.astype(BF16), zgap,
                                    meta_ref[s].astype(BF16), zp], axis=0)
        return get

    rows = lambda s: s * l
    attn_items = _attention_items(ns, l, lambda s: z_ref[rows(s):rows(s) + l, _Q0:_Q0 + D_ATTN],
                                  key_slots(kwin_ref, kmeta_ref, _K0), key_slots(vwin_ref, vmeta_ref, _V0),
                                  lambda s: bias_ref[0], sinks_ref, _put_heads(mix_ref, rows, l))
    _run(_interleave(conv_items, attn_items))

    _run(_out_proj_items(x_ref, mix_ref, w_out_ref, gffn_ref, x1_ref, hb_ref, ns, l, 256))

    nprev = FFN_CONV_WIDTH - 1
    for i in range(2):
        for s in range(ns):
            gbuf[i, s, 0:8, :] = jnp.zeros((8, FFN_CW), F32)

    def put_halo(gb, s, cols):
        gb[s, 8 - nprev:8, :] = ffn_ref[s, :, cols]

    def save_carry(gb, s, cols):
        ffn_o[s, :, cols] = gb[s, 8 + l - nprev:8 + l, :]

    _run(_ffn_pipelined(*_ffn_items(hb_ref, w_up_ref, fw_ref, fb_ref, gbuf, ubuf, yh_ref, put_halo, save_carry,
                                    ns, l, l)))
    _run(_ffn_out_items(x1_ref, yh_ref, w_down_ref, gfin_ref, y_o, ns, l, 256))


def _run_sample(sinks, x, kmeta, vmeta, kwin, vwin, conva, ffn, bias_s, weights):
    nstream, l, _ = x.shape
    assert nstream % SB == 0 and l % 16 == 0 and l <= CHUNK and kwin.shape[1] == WINDOW
    rows = SB * l
    per_stream = [x, kmeta, vmeta, kwin, vwin, conva, ffn]
    consts = [bias_s, *weights]
    in_specs = ([pl.BlockSpec(memory_space=pltpu.SMEM)]
                + [pl.BlockSpec((SB,) + a.shape[1:], lambda i: (i, 0, 0)) for a in per_stream]
                + [_const_spec(a) for a in consts])
    out_shape = [jax.ShapeDtypeStruct(a.shape, F32) for a in (x, kwin, vwin, conva, ffn)]
    out_specs = [pl.BlockSpec((SB,) + s.shape[1:], lambda i: (i, 0, 0)) for s in out_shape]
    scratch = [pltpu.VMEM((rows, D_MODEL), BF16), pltpu.VMEM((rows, D_IN), F32),
               pltpu.VMEM((HALO + l, D_CONV), F32),
               pltpu.VMEM((rows, D_MODEL), BF16), pltpu.VMEM((rows, D_MODEL), F32),
               pltpu.VMEM((2, SB, 8 + l, FFN_CW), F32), pltpu.VMEM((2, rows, FFN_CW), F32),
               pltpu.VMEM((rows, D_FF), BF16)]
    return pl.pallas_call(
        _sample_kernel,
        grid=(nstream // SB,),
        out_shape=out_shape,
        in_specs=in_specs,
        out_specs=out_specs,
        scratch_shapes=scratch,
        compiler_params=pltpu.CompilerParams(dimension_semantics=("arbitrary",), vmem_limit_bytes=VMEM_LIMIT),
        name="sample_step",
    )(sinks, *per_stream, *consts)


def kernel(x_prompt, x_sample, cache_k_meta, cache_v_meta, cache_k_win, cache_v_win, state_conv_a, state_conv_ffn,
           meta_tokens, rel_bias_table, norm_mix, w_in, conv_dw_w, conv_dw_b, conv_ln_g, conv_ln_b, attn_sinks,
           w_out, norm_ffn, w_up, ffn_dw_w, ffn_dw_b, w_down, norm_final):
    depth = w_in.shape[0]
    assert depth == 1, "single-layer step"
    bsz = x_prompt.shape[0]
    nstream, dec_seq, _ = x_sample.shape
    win = cache_k_win.shape[2]
    assert win == WINDOW

    bp, bs, bm = _bucket_tables(PAST_LEN, dec_seq)
    bias_p, bias_s, bias_m = _build_bias(rel_bias_table, jnp.asarray(bp), jnp.asarray(bs), jnp.asarray(bm))

    row = lambda a: a.reshape(1, -1)
    w_in_b = w_in[0].astype(BF16)
    w_out_b = w_out[0].astype(BF16)
    w_up_b = w_up[0].astype(BF16)
    w_down_b = w_down[0].astype(BF16)
    sinks = attn_sinks[0]
    gmix, gffn, gfin = row(norm_mix[0]), row(norm_ffn[0]), row(norm_final)
    cw = jnp.broadcast_to(conv_dw_w[0][::-1][:, None, :], (CONV_WIDTH, 8, D_CONV))
    dw_b = row(conv_dw_b[0])
    ln_g, ln_b = row(conv_ln_g[0]), row(conv_ln_b[0])
    fw = jnp.broadcast_to(ffn_dw_w[0][:, None, :], (FFN_CONV_WIDTH, 8, D_FF))
    fb = jnp.broadcast_to(ffn_dw_b[0][None, :], (8, D_FF))

    glu_m, k_m, v_m, gate_m = _run_meta(sinks, meta_tokens, bias_m, gmix, w_in_b, cw, dw_b, ln_g, ln_b,
                                        w_out_b, gffn, w_up_b)

    weights = [gmix, w_in_b, cw, dw_b, ln_g, ln_b, w_out_b, gffn, w_up_b, fw, fb, w_down_b, gfin]
    y_p, kwin_p, vwin_p, conva_p, ffn_p = _run_prompt(sinks, x_prompt, glu_m, k_m, v_m, gate_m, bias_p, weights)

    kv = lambda a: a.reshape(a.shape[0], a.shape[1], D_KV)
    y_s, kwin_s, vwin_s, conva_s, ffn_s = _run_sample(
        sinks, x_sample, kv(cache_k_meta[0]), kv(cache_v_meta[0]), kv(cache_k_win[0]), kv(cache_v_win[0]),
        state_conv_a[0], state_conv_ffn[0], bias_s, weights)

    heads = lambda a: a.reshape(1, a.shape[0], a.shape[1], N_KV_HEADS, HEAD_DIM)
    meta_b = lambda a: jnp.broadcast_to(a.reshape(1, 1, N_META, N_KV_HEADS, HEAD_DIM),
                                        (1, bsz, N_META, N_KV_HEADS, HEAD_DIM))
    return (y_p, y_s, meta_b(k_m), meta_b(v_m), heads(kwin_p), heads(vwin_p), conva_p[None], ffn_p[None],
            heads(kwin_s), heads(vwin_s), conva_s[None], ffn_s[None])
```

```python
import functools
import math

import numpy as np
import jax
import jax.numpy as jnp
from jax import lax
from jax.experimental import pallas as pl
from jax.experimental.pallas import tpu as pltpu

F32 = jnp.float32
BF16 = jnp.bfloat16

D_MODEL = 1024
CHUNK = 64
N_META = 16
D_CONV = 512
CONV_WIDTH = 31
N_HEADS = 8
N_KV_HEADS = 2
HEAD_DIM = 64
HEADS_PER_KV = N_HEADS // N_KV_HEADS
D_ATTN = N_HEADS * HEAD_DIM
D_KV = N_KV_HEADS * HEAD_DIM
WINDOW = 128
PAST_LEN = 4096
N_BUCKETS = 32
MAX_DISTANCE = 256
D_FF = 2816
FFN_CONV_WIDTH = 3
D_IN = 2 * D_CONV + D_ATTN + 2 * D_KV
EPS = 1e-6
NEG_INF = -1e30
SCALE = HEAD_DIM ** -0.5

_Q0 = 2 * D_CONV
_K0 = _Q0 + D_ATTN
_V0 = _K0 + D_KV

NKEY = 256
BAND = WINDOW + CHUNK
MAX_STACKED_ROWS = 256
HALO = 32
FFN_CW = 256
TM = 512
SB = 16
VMEM_LIMIT = 60 * 1024 * 1024


def _np_bucket(rel):
    nb = N_BUCKETS // 2
    max_exact = nb // 2
    ret = np.where(rel > 0, nb, 0)
    n = np.abs(rel)
    nf = np.maximum(n, 1).astype(np.float32)
    large = max_exact + (np.log(nf / np.float32(max_exact)) / np.float32(math.log(MAX_DISTANCE / max_exact))
                         * np.float32(nb - max_exact)).astype(np.int32)
    large = np.minimum(large, nb - 1)
    return (ret + np.where(n < max_exact, n, large)).astype(np.int32)


def _bucket_tables(past_len, dec_seq):
    i = np.arange(CHUNK)[:, None]
    j = np.arange(BAND)[None, :]
    m = np.arange(N_META)[None, :]
    band = _np_bucket(j - WINDOW - i)
    pad = np.full((CHUNK, NKEY - BAND - N_META), -1, np.int32)

    def meta_for(c):
        return _np_bucket(m - (N_META + CHUNK * c + i))

    prompt = []
    for c in range(4):
        b = np.where(c - 2 + j // CHUNK >= 0, band, -1)
        prompt.append(np.concatenate([b, meta_for(c), pad], axis=1))
    assert np.array_equal(meta_for(3), meta_for(4096))
    prompt = np.stack(prompt).astype(np.int32)

    i_s = np.arange(dec_seq)[:, None]
    band_s = np.where(j < WINDOW + dec_seq, _np_bucket(j - WINDOW - i_s), -1)
    meta_s = _np_bucket(m - (N_META + past_len + i_s))
    sample = np.concatenate([band_s, meta_s, pad[:dec_seq]], axis=1)[None].astype(np.int32)

    i_m = np.arange(N_META)[:, None]
    meta_self = np.concatenate([np.full((N_META, BAND), -1, np.int32), _np_bucket(m - i_m), pad[:N_META]],
                               axis=1)[None].astype(np.int32)
    return prompt, sample, meta_self


def _bias_kernel(table_ref, bp_ref, bs_ref, bm_ref, op_ref, os_ref, om_ref):
    for b_ref, o_ref in ((bp_ref, op_ref), (bs_ref, os_ref), (bm_ref, om_ref)):
        nv, r, _ = b_ref.shape
        for v in range(nv):
            bk = b_ref[v]
            for h in range(N_HEADS):
                acc = jnp.full((r, NKEY), NEG_INF, F32)
                for b in range(N_BUCKETS):
                    acc = jnp.where(bk == b, table_ref[b, h], acc)
                o_ref[v, r * h:r * (h + 1), :] = acc


def _build_bias(table, bp, bs, bm):
    shapes = [jax.ShapeDtypeStruct((b.shape[0], N_HEADS * b.shape[1], NKEY), F32) for b in (bp, bs, bm)]
    vm = pl.BlockSpec(memory_space=pltpu.VMEM)
    return pl.pallas_call(
        _bias_kernel,
        out_shape=shapes,
        in_specs=[pl.BlockSpec(memory_space=pltpu.SMEM), vm, vm, vm],
        out_specs=[vm, vm, vm],
        name="rel_bias",
    )(table, bp, bs, bm)


def _dot(a, b):
    return jnp.dot(a, b, preferred_element_type=F32)


def _dot_nt(a, b):
    return lax.dot_general(a, b, (((1,), (1,)), ((), ())), preferred_element_type=F32)


def _rms(x, g):
    ms = jnp.mean(x * x, axis=-1, keepdims=True)
    return (x * lax.rsqrt(ms + EPS)) * g


def _zero_from(x):
    u = lax.bitcast_convert_type(x, jnp.int32)
    u = lax.shift_right_logical(lax.shift_right_logical(u, jnp.int32(16)), jnp.int32(16))
    return u.astype(F32)


def _swap_halves(x):
    return pltpu.roll(x, HEAD_DIM, axis=1)


def _attention_items(nblk, r, get_q, get_k, get_v, get_bias, sinks_ref, put_out):
    hp = min(N_HEADS, max(HEADS_PER_KV, MAX_STACKED_ROWS // r))

    def block(b):
        q = get_q(b) * SCALE
        kb = get_k(b)
        vb = get_v(b)
        lo = lax.broadcasted_iota(jnp.int32, (r, 2 * HEAD_DIM), 1) < HEAD_DIM
        hi = jnp.logical_not(lo)
        bias = get_bias(b)
        for h0 in range(0, N_HEADS, hp):
            parts = []
            for h in range(h0, h0 + hp):
                g = h // HEADS_PER_KV
                qh = q[:, 128 * (h // 2):128 * (h // 2) + 128]
                if h % 2 != g:
                    qh = _swap_halves(qh)
                parts.append(jnp.where(hi if g else lo, qh, 0.0))
            qs = jnp.concatenate(parts, axis=0).astype(BF16)
            s = _dot_nt(qs, kb) + bias[r * h0:r * (h0 + hp), :]
            probs = []
            for j in range(hp):
                sink = sinks_ref[h0 + j]
                seg = s[r * j:r * (j + 1), :]
                mx = jnp.maximum(jnp.max(seg, axis=-1, keepdims=True), sink)
                e = jnp.exp(seg - mx)
                den = jnp.sum(e, axis=-1, keepdims=True) + jnp.exp(sink - mx)
                probs.append((e * (1.0 / den)).astype(BF16))
            o = _dot(jnp.concatenate(probs, axis=0), vb)
            for j in range(0, hp, 2):
                g = (h0 + j) // HEADS_PER_KV
                even = o[r * j:r * (j + 1)]
                odd = o[r * (j + 1):r * (j + 2)]
                if g == 1:
                    even = _swap_halves(even)
                else:
                    odd = _swap_halves(odd)
                put_out(b, (h0 + j) // 2, jnp.where(lo, even, odd))

    return [functools.partial(block, b) for b in range(nblk)]


def _put_heads(mix_ref, row0, r):
    def put(b, p, o):
        base = D_CONV + 128 * p
        mix_ref[row0(b):row0(b) + r, base:base + 128] = o.astype(BF16)
    return put


def _in_proj(x_ref, gmix_ref, w_in_ref, hb_ref, z_ref, ns, l, rc):
    g = gmix_ref[...]
    for s in range(ns):
        for r0 in range(0, l, rc):
            hb_ref[s * l + r0:s * l + r0 + rc, :] = _rms(x_ref[s, r0:r0 + rc, :], g).astype(BF16)
    z_ref[:, 0:_Q0] = _dot(hb_ref[...], w_in_ref[:, 0:_Q0])
    z_ref[:, _Q0:D_IN] = _dot(hb_ref[...], w_in_ref[:, _Q0:D_IN])


def _delay(x):
    return pltpu.roll(x, 1, axis=0)


def _conv31_block(glu_buf, u0, rc, cw_ref, l0, n):
    nb = (rc + 8) // 8
    acc = None
    for r in range(7, -1, -1):
        s = None
        for a in range(4):
            d = 8 * a + r
            if d >= CONV_WIDTH:
                continue
            blk = glu_buf[u0 - 8 - 8 * a:u0 + rc - 8 * a, l0:l0 + n].reshape(nb, 8, n)
            term = blk * cw_ref[d, :, l0:l0 + n][None]
            s = term if s is None else s + term
        s = s.reshape(rc + 8, n)
        acc = s if acc is None else s + _delay(acc)
    return acc[8:]


def _run(items):
    for item in items:
        item()


def _interleave(a, b):
    out, ia, ib = [], 0, 0
    while ia < len(a) or ib < len(b):
        if ib >= len(b) or (ia < len(a) and ia * len(b) <= ib * len(a)):
            out.append(a[ia])
            ia += 1
        else:
            out.append(b[ib])
            ib += 1
    return out


def _conv_items(z_ref, row0, l, glu_buf, cw_ref, dw_b_ref, ln_g_ref, ln_b_ref, mix_ref, rc):
    rn = min(rc, 64)

    def glu(r0):
        a = z_ref[row0 + r0:row0 + r0 + rc, 0:D_CONV]
        b = z_ref[row0 + r0:row0 + r0 + rc, D_CONV:2 * D_CONV]
        glu_buf[HALO + r0:HALO + r0 + rc, :] = a * jax.nn.sigmoid(b)

    def conv(r0):
        for l0 in range(0, D_CONV, 128):
            z_ref[row0 + r0:row0 + r0 + rc, l0:l0 + 128] = _conv31_block(glu_buf, HALO + r0, rc, cw_ref, l0, 128)

    def norm(r0):
        for q0 in range(r0, r0 + rc, rn):
            acc = z_ref[row0 + q0:row0 + q0 + rn, 0:D_CONV] + dw_b_ref[...]
            mu = jnp.mean(acc, axis=-1, keepdims=True)
            d = acc - mu
            var = jnp.mean(d * d, axis=-1, keepdims=True)
            yn = (d * lax.rsqrt(var + EPS)) * ln_g_ref[...] + ln_b_ref[...]
            mix_ref[row0 + q0:row0 + q0 + rn, 0:D_CONV] = (yn * jax.nn.sigmoid(yn)).astype(BF16)

    items = []
    for r0 in range(0, l, rc):
        items += [functools.partial(glu, r0), functools.partial(conv, r0), functools.partial(norm, r0)]
    return items


def _seq_pieces(m0, mc, l):
    pc = min(l, mc)
    return [(p0,) + divmod(m0 + p0, l) + (pc,) for p0 in range(0, mc, pc)]


def _out_proj_items(x_ref, mix_ref, w_out_ref, gffn_ref, x1_ref, hb_ref, ns, l, mc):
    def chunk(m0):
        acc = _dot(mix_ref[m0:m0 + mc, :], w_out_ref[...])
        for p0, s, r0, pc in _seq_pieces(m0, mc, l):
            x1 = x_ref[s, r0:r0 + pc, :] + acc[p0:p0 + pc]
            x1_ref[m0 + p0:m0 + p0 + pc, :] = x1
            hb_ref[m0 + p0:m0 + p0 + pc, :] = _rms(x1, gffn_ref[...]).astype(BF16)

    return [functools.partial(chunk, m0) for m0 in range(0, ns * l, mc)]


def _ffn_items(hb_ref, w_up_ref, fw_ref, fb_ref, gbuf, ubuf, y_ref, put_halo, save_carry, ns, l, rc):
    nch = D_FF // FFN_CW
    nb = (rc + 8) // 8

    def matmuls(ci):
        c = ci * FFN_CW
        gb = gbuf.at[ci % 2]
        ug = _dot(hb_ref[...], w_up_ref[:, c:c + FFN_CW])
        for s in range(ns):
            put_halo(gb, s, slice(c, c + FFN_CW))
            gb[s, 8:8 + l, :] = ug[s * l:(s + 1) * l]
        ubuf[ci % 2] = _dot(hb_ref[...], w_up_ref[:, D_FF + c:D_FF + c + FFN_CW])

    def elementwise(ci):
        c = ci * FFN_CW
        gb = gbuf.at[ci % 2]
        ub = ubuf.at[ci % 2]
        cols = slice(c, c + FFN_CW)
        w0 = fw_ref[0, :, cols][None]
        w1 = fw_ref[1, :, cols][None]
        w2 = fw_ref[2, :, cols][None]
        bb = fb_ref[:, cols][None]
        for s in range(ns):
            for r0 in range(0, l, rc):
                g = gb[s, r0:r0 + rc + 8, :].reshape(nb, 8, FFN_CW)
                t = (g * w0).reshape(rc + 8, FFN_CW)
                t = (g * w1).reshape(rc + 8, FFN_CW) + _delay(t)
                t = (g * w2 + bb).reshape(rc + 8, FFN_CW) + _delay(t)
                cv = t[8:]
                y = (cv * jax.nn.sigmoid(cv)) * ub[s * l + r0:s * l + r0 + rc, :]
                y_ref[s * l + r0:s * l + r0 + rc, cols] = y.astype(BF16)
            save_carry(gb, s, cols)

    return ([functools.partial(matmuls, ci) for ci in range(nch)],
            [functools.partial(elementwise, ci) for ci in range(nch)])


def _ffn_pipelined(mm, el):
    items = [mm[0]]
    for ci in range(len(el)):
        if ci + 1 < len(mm):
            items.append(mm[ci + 1])
        items.append(el[ci])
    return items


def _ffn_out_items(x1_ref, y_ref, w_down_ref, gfin_ref, o_ref, ns, l, mc):
    def chunk(m0):
        acc = _dot(y_ref[m0:m0 + mc, :], w_down_ref[...])
        for p0, s, r0, pc in _seq_pieces(m0, mc, l):
            x2 = x1_ref[m0 + p0:m0 + p0 + pc, :] + acc[p0:p0 + pc]
            o_ref[s, r0:r0 + pc, :] = _rms(x2, gfin_ref[...])

    return [functools.partial(chunk, m0) for m0 in range(0, ns * l, mc)]


def _meta_kernel(sinks_ref, x_ref, bias_ref, gmix_ref, w_in_ref, cw_ref, dw_b_ref, ln_g_ref, ln_b_ref,
                 w_out_ref, gffn_ref, w_upg_ref,
                 glu_o, k_o, v_o, gate_o,
                 hb_ref, z_ref, glu_buf, mix_ref):
    l = N_META
    _in_proj(x_ref, gmix_ref, w_in_ref, hb_ref, z_ref, 1, l, l)
    glu_buf[0:HALO, :] = jnp.zeros((HALO, D_CONV), F32)
    _run(_conv_items(z_ref, 0, l, glu_buf, cw_ref, dw_b_ref, ln_g_ref, ln_b_ref, mix_ref, l))
    glu_o[...] = glu_buf[HALO:HALO + l, :]
    k = z_ref[:, _K0:_K0 + D_KV]
    v = z_ref[:, _V0:_V0 + D_KV]
    k_o[...] = k
    v_o[...] = v
    zb = jnp.zeros((BAND, D_KV), BF16)
    zp = jnp.zeros((NKEY - BAND - N_META, D_KV), BF16)
    kb = jnp.concatenate([zb, k.astype(BF16), zp], axis=0)
    vb = jnp.concatenate([zb, v.astype(BF16), zp], axis=0)
    _run(_attention_items(1, l, lambda b: z_ref[:, _Q0:_Q0 + D_ATTN], lambda b: kb, lambda b: vb,
                          lambda b: bias_ref[0], sinks_ref, _put_heads(mix_ref, lambda b: 0, l)))
    x1 = x_ref[0] + _dot(mix_ref[...], w_out_ref[...])
    h2 = _rms(x1, gffn_ref[...]).astype(BF16)
    gate = _dot(h2, w_upg_ref[...])
    gate_o[...] = gate[N_META - 8:N_META, :]


def _run_meta(sinks, meta_tokens, bias_m, gmix, w_in, cw, dw_b, ln_g, ln_b, w_out, gffn, w_up):
    l = N_META
    full = lambda a: pl.BlockSpec(a.shape, lambda i: (0,) * a.ndim)
    x = meta_tokens[None]
    ins = [x, bias_m, gmix, w_in, cw, dw_b, ln_g, ln_b, w_out, gffn]
    in_specs = [pl.BlockSpec(memory_space=pltpu.SMEM)] + [full(a) for a in ins]
    in_specs.append(pl.BlockSpec((D_MODEL, D_FF), lambda i: (0, 0)))
    out_shape = [jax.ShapeDtypeStruct((l, D_CONV), F32), jax.ShapeDtypeStruct((l, D_KV), F32),
                 jax.ShapeDtypeStruct((l, D_KV), F32), jax.ShapeDtypeStruct((8, D_FF), F32)]
    return pl.pallas_call(
        _meta_kernel,
        grid=(1,),
        out_shape=out_shape,
        in_specs=in_specs,
        out_specs=[pl.BlockSpec(s.shape, lambda i: (0, 0)) for s in out_shape],
        scratch_shapes=[pltpu.VMEM((l, D_MODEL), BF16), pltpu.VMEM((l, D_IN), F32),
                        pltpu.VMEM((HALO + l, D_CONV), F32), pltpu.VMEM((l, D_MODEL), BF16)],
        compiler_params=pltpu.CompilerParams(vmem_limit_bytes=VMEM_LIMIT),
        name="meta_tokens",
    )(sinks, *ins, w_up)


def _prompt_kernel(sinks_ref, x_ref, glu_m_ref, k_m_ref, v_m_ref, gate_m_ref, bias_ref,
                   gmix_ref, w_in_ref, cw_ref, dw_b_ref, ln_g_ref, ln_b_ref, w_out_ref, gffn_ref,
                   w_up_ref, fw_ref, fb_ref, w_down_ref, gfin_ref,
                   y_o, kwin_o, vwin_o, conva_o, ffn_o,
                   hb_ref, z_ref, glu_buf, kv_buf, mix_ref, x1_ref, gbuf, ubuf, yh_ref, gcarry):
    t = pl.program_id(1)
    l = TM
    rc = 128
    nrc = l // rc

    @pl.when(t == 0)
    def _():
        glu_buf[0:HALO - N_META, :] = jnp.zeros((HALO - N_META, D_CONV), F32)
        glu_buf[HALO - N_META:HALO, :] = glu_m_ref[...]
        kv_buf[:, 0:WINDOW, :] = jnp.zeros((2, WINDOW, D_KV), BF16)
        gcarry[...] = gate_m_ref[...]

    def conv_done(j):
        return _zero_from(z_ref[j * rc:j * rc + 8, 0:128])

    def in_rows(j, behind=None):
        r0 = j * rc
        g = gmix_ref[...]
        if behind is not None:
            g = g + jnp.tile(conv_done(behind)[0:1, :], (1, D_MODEL // 128))
        hb_ref[r0:r0 + rc, :] = _rms(x_ref[0, r0:r0 + rc, :], g).astype(BF16)
        z_ref[r0:r0 + rc, 0:_Q0] = _dot(hb_ref[r0:r0 + rc, :], w_in_ref[:, 0:_Q0])
        z_ref[r0:r0 + rc, _Q0:D_IN] = _dot(hb_ref[r0:r0 + rc, :], w_in_ref[:, _Q0:D_IN])
        for idx, c0 in enumerate((_K0, _V0)):
            kv_buf[idx, WINDOW + r0:WINDOW + r0 + rc, :] = z_ref[r0:r0 + rc, c0:c0 + D_KV].astype(BF16)

    zp = jnp.zeros((NKEY - BAND - N_META, D_KV), BF16)
    km = jnp.concatenate([k_m_ref[...].astype(BF16), zp], axis=0)
    vm = jnp.concatenate([v_m_ref[...].astype(BF16), zp], axis=0)

    def variant(c):
        return jnp.where(t == 0, c, 3) if c < 3 else 3

    rows = lambda c: c * CHUNK
    cpr = rc // CHUNK

    def get_q(c):
        q = z_ref[rows(c):rows(c) + CHUNK, _Q0:_Q0 + D_ATTN]
        return q + jnp.tile(conv_done(min(c // cpr, nrc - 2)), (CHUNK // 8, D_ATTN // 128))

    attn_items = _attention_items(
        l // CHUNK, CHUNK, get_q,
        lambda c: jnp.concatenate([kv_buf[0, rows(c):rows(c) + BAND, :], km], axis=0),
        lambda c: jnp.concatenate([kv_buf[1, rows(c):rows(c) + BAND, :], vm], axis=0),
        lambda c: bias_ref[variant(c)],
        sinks_ref, _put_heads(mix_ref, rows, CHUNK))

    conv_items = _conv_items(z_ref, 0, l, glu_buf, cw_ref, dw_b_ref, ln_g_ref, ln_b_ref, mix_ref, rc)
    per = len(conv_items) // nrc
    mixer = [functools.partial(in_rows, 0), functools.partial(in_rows, 1)]
    for j in range(nrc):
        if j >= 1 and j + 1 < nrc:
            mixer.append(functools.partial(in_rows, j + 1, behind=j - 1))
        if j >= 1:
            mixer += attn_items[cpr * (j - 1):cpr * j if j + 1 < nrc else l // CHUNK]
        mixer += conv_items[per * j:per * (j + 1)]

    def carry_state():
        glu_buf[0:HALO, :] = glu_buf[l:l + HALO, :]
        kv_buf[:, 0:WINDOW, :] = kv_buf[:, l:l + WINDOW, :]

    _run(mixer)
    for win_o, c0 in ((kwin_o, _K0), (vwin_o, _V0)):
        win_o[0] = z_ref[l - WINDOW:l, c0:c0 + D_KV]
    conva_o[0] = glu_buf[HALO + l - (CONV_WIDTH - 1):HALO + l, :]
    carry_state()
    _run(_out_proj_items(x_ref, mix_ref, w_out_ref, gffn_ref, x1_ref, hb_ref, 1, l, 256))

    def put_halo(gb, s, cols):
        gb[s, 0:8, :] = gcarry[:, cols]

    def save_carry(gb, s, cols):
        gcarry[:, cols] = gb[s, l:l + 8, :]

    _run(_ffn_pipelined(*_ffn_items(hb_ref, w_up_ref, fw_ref, fb_ref, gbuf, ubuf, yh_ref, put_halo, save_carry,
                                    1, l, rc)))
    ffn_o[0] = gcarry[8 - (FFN_CONV_WIDTH - 1):8, :]
    _run(_ffn_out_items(x1_ref, yh_ref, w_down_ref, gfin_ref, y_o, 1, l, 256))


def _const_spec(a):
    return pl.BlockSpec(a.shape, lambda *_: (0,) * a.ndim, pipeline_mode=pl.Buffered(1))


def _run_prompt(sinks, x, glu_m, k_m, v_m, gate_m, bias_p, weights):
    bsz, seq, _ = x.shape
    assert seq % TM == 0 and TM % CHUNK == 0 and TM >= WINDOW
    consts = [glu_m, k_m, v_m, gate_m, bias_p, *weights]
    in_specs = ([pl.BlockSpec(memory_space=pltpu.SMEM), pl.BlockSpec((1, TM, D_MODEL), lambda b, t: (b, t, 0))]
                + [_const_spec(a) for a in consts])
    out_shape = [jax.ShapeDtypeStruct((bsz, seq, D_MODEL), F32),
                 jax.ShapeDtypeStruct((bsz, WINDOW, D_KV), F32), jax.ShapeDtypeStruct((bsz, WINDOW, D_KV), F32),
                 jax.ShapeDtypeStruct((bsz, CONV_WIDTH - 1, D_CONV), F32),
                 jax.ShapeDtypeStruct((bsz, FFN_CONV_WIDTH - 1, D_FF), F32)]
    out_specs = [pl.BlockSpec((1, TM, D_MODEL), lambda b, t: (b, t, 0))] + [
        pl.BlockSpec((1,) + s.shape[1:], lambda b, t: (b, 0, 0)) for s in out_shape[1:]]
    scratch = [pltpu.VMEM((TM, D_MODEL), BF16), pltpu.VMEM((TM, D_IN), F32),
               pltpu.VMEM((HALO + TM, D_CONV), F32), pltpu.VMEM((2, WINDOW + TM, D_KV), BF16),
               pltpu.VMEM((TM, D_MODEL), BF16), pltpu.VMEM((TM, D_MODEL), F32),
               pltpu.VMEM((2, 1, 8 + TM, FFN_CW), F32), pltpu.VMEM((2, TM, FFN_CW), F32),
               pltpu.VMEM((TM, D_FF), BF16), pltpu.VMEM((8, D_FF), F32)]
    return pl.pallas_call(
        _prompt_kernel,
        grid=(bsz, seq // TM),
        out_shape=out_shape,
        in_specs=in_specs,
        out_specs=out_specs,
        scratch_shapes=scratch,
        compiler_params=pltpu.CompilerParams(dimension_semantics=("arbitrary", "arbitrary"),
                                             vmem_limit_bytes=VMEM_LIMIT),
        name="prompt_step",
    )(sinks, x, *consts)


def _sample_kernel(sinks_ref, x_ref, kmeta_ref, vmeta_ref, kwin_ref, vwin_ref, conva_ref, ffn_ref, bias_ref,
                   gmix_ref, w_in_ref, cw_ref, dw_b_ref, ln_g_ref, ln_b_ref, w_out_ref, gffn_ref,
                   w_up_ref, fw_ref, fb_ref, w_down_ref, gfin_ref,
                   y_o, kwin_o, vwin_o, conva_o, ffn_o,
                   hb_ref, z_ref, glu_buf, mix_ref, x1_ref, gbuf, ubuf, yh_ref):
    ns, l, _ = x_ref.shape
    _in_proj(x_ref, gmix_ref, w_in_ref, hb_ref, z_ref, ns, l, l)
    glu_buf[0:HALO - 30, :] = jnp.zeros((HALO - 30, D_CONV), F32)
    zgap = jnp.zeros((BAND - WINDOW - l, D_KV), BF16)
    zp = jnp.zeros((NKEY - BAND - N_META, D_KV), BF16)
    def load_state(s):
        glu_buf[HALO - (CONV_WIDTH - 1):HALO, :] = conva_ref[s]

    def save_state(s):
        conva_o[s] = glu_buf[HALO + l - (CONV_WIDTH - 1):HALO + l, :]
        for win_ref, win_o, c0 in ((kwin_ref, kwin_o, _K0), (vwin_ref, vwin_o, _V0)):
            win_o[s, 0:WINDOW - l, :] = win_ref[s, l:WINDOW, :]
            win_o[s, WINDOW - l:WINDOW, :] = z_ref[s * l:(s + 1) * l, c0:c0 + D_KV]

    conv_items = []
    for s in range(ns):
        conv_items += ([functools.partial(load_state, s)]
                       + _conv_items(z_ref, s * l, l, glu_buf, cw_ref, dw_b_ref, ln_g_ref, ln_b_ref, mix_ref, l)
                       + [functools.partial(save_state, s)])

    def key_slots(win_ref, meta_ref, c0):
        def get(s):
            new = z_ref[s * l:(s + 1) * l, c0:c0 + D_KV]
            return jnp.concatenate([win_ref[s].astype(BF16), new.astype(BF16), zgap,
                                    meta_ref[s].astype(BF16), zp], axis=0)
        return get

    rows = lambda s: s * l
    attn_items = _attention_items(ns, l, lambda s: z_ref[rows(s):rows(s) + l, _Q0:_Q0 + D_ATTN],
                                  key_slots(kwin_ref, kmeta_ref, _K0), key_slots(vwin_ref, vmeta_ref, _V0),
                                  lambda s: bias_ref[0], sinks_ref, _put_heads(mix_ref, rows, l))
    _run(_interleave(conv_items, attn_items))

    _run(_out_proj_items(x_ref, mix_ref, w_out_ref, gffn_ref, x1_ref, hb_ref, ns, l, 256))

    nprev = FFN_CONV_WIDTH - 1
    for i in range(2):
        for s in range(ns):
            gbuf[i, s, 0:8, :] = jnp.zeros((8, FFN_CW), F32)

    def put_halo(gb, s, cols):
        gb[s, 8 - nprev:8, :] = ffn_ref[s, :, cols]

    def save_carry(gb, s, cols):
        ffn_o[s, :, cols] = gb[s, 8 + l - nprev:8 + l, :]

    _run(_ffn_pipelined(*_ffn_items(hb_ref, w_up_ref, fw_ref, fb_ref, gbuf, ubuf, yh_ref, put_halo, save_carry,
                                    ns, l, l)))
    _run(_ffn_out_items(x1_ref, yh_ref, w_down_ref, gfin_ref, y_o, ns, l, 256))


def _run_sample(sinks, x, kmeta, vmeta, kwin, vwin, conva, ffn, bias_s, weights):
    nstream, l, _ = x.shape
    assert nstream % SB == 0 and l % 16 == 0 and l <= CHUNK and kwin.shape[1] == WINDOW
    rows = SB * l
    per_stream = [x, kmeta, vmeta, kwin, vwin, conva, ffn]
    consts = [bias_s, *weights]
    in_specs = ([pl.BlockSpec(memory_space=pltpu.SMEM)]
                + [pl.BlockSpec((SB,) + a.shape[1:], lambda i: (i, 0, 0)) for a in per_stream]
                + [_const_spec(a) for a in consts])
    out_shape = [jax.ShapeDtypeStruct(a.shape, F32) for a in (x, kwin, vwin, conva, ffn)]
    out_specs = [pl.BlockSpec((SB,) + s.shape[1:], lambda i: (i, 0, 0)) for s in out_shape]
    scratch = [pltpu.VMEM((rows, D_MODEL), BF16), pltpu.VMEM((rows, D_IN), F32),
               pltpu.VMEM((HALO + l, D_CONV), F32),
               pltpu.VMEM((rows, D_MODEL), BF16), pltpu.VMEM((rows, D_MODEL), F32),
               pltpu.VMEM((2, SB, 8 + l, FFN_CW), F32), pltpu.VMEM((2, rows, FFN_CW), F32),
               pltpu.VMEM((rows, D_FF), BF16)]
    return pl.pallas_call(
        _sample_kernel,
        grid=(nstream // SB,),
        out_shape=out_shape,
        in_specs=in_specs,
        out_specs=out_specs,
        scratch_shapes=scratch,
        compiler_params=pltpu.CompilerParams(dimension_semantics=("arbitrary",), vmem_limit_bytes=VMEM_LIMIT),
        name="sample_step",
    )(sinks, *per_stream, *consts)


def kernel(x_prompt, x_sample, cache_k_meta, cache_v_meta, cache_k_win, cache_v_win, state_conv_a, state_conv_ffn,
           meta_tokens, rel_bias_table, norm_mix, w_in, conv_dw_w, conv_dw_b, conv_ln_g, conv_ln_b, attn_sinks,
           w_out, norm_ffn, w_up, ffn_dw_w, ffn_dw_b, w_down, norm_final):
    depth = w_in.shape[0]
    assert depth == 1, "single-layer step"
    bsz = x_prompt.shape[0]
    nstream, dec_seq, _ = x_sample.shape
    win = cache_k_win.shape[2]
    assert win == WINDOW

    bp, bs, bm = _bucket_tables(PAST_LEN, dec_seq)
    bias_p, bias_s, bias_m = _build_bias(rel_bias_table, jnp.asarray(bp), jnp.asarray(bs), jnp.asarray(bm))

    row = lambda a: a.reshape(1, -1)
    w_in_b = w_in[0].astype(BF16)
    w_out_b = w_out[0].astype(BF16)
    w_up_b = w_up[0].astype(BF16)
    w_down_b = w_down[0].astype(BF16)
    sinks = attn_sinks[0]
    gmix, gffn, gfin = row(norm_mix[0]), row(norm_ffn[0]), row(norm_final)
    cw = jnp.broadcast_to(conv_dw_w[0][::-1][:, None, :], (CONV_WIDTH, 8, D_CONV))
    dw_b = row(conv_dw_b[0])
    ln_g, ln_b = row(conv_ln_g[0]), row(conv_ln_b[0])
    fw = jnp.broadcast_to(ffn_dw_w[0][:, None, :], (FFN_CONV_WIDTH, 8, D_FF))
    fb = jnp.broadcast_to(ffn_dw_b[0][None, :], (8, D_FF))

    glu_m, k_m, v_m, gate_m = _run_meta(sinks, meta_tokens, bias_m, gmix, w_in_b, cw, dw_b, ln_g, ln_b,
                                        w_out_b, gffn, w_up_b)

    weights = [gmix, w_in_b, cw, dw_b, ln_g, ln_b, w_out_b, gffn, w_up_b, fw, fb, w_down_b, gfin]
    y_p, kwin_p, vwin_p, conva_p, ffn_p = _run_prompt(sinks, x_prompt, glu_m, k_m, v_m, gate_m, bias_p, weights)

    kv = lambda a: a.reshape(a.shape[0], a.shape[1], D_KV)
    y_s, kwin_s, vwin_s, conva_s, ffn_s = _run_sample(
        sinks, x_sample, kv(cache_k_meta[0]), kv(cache_v_meta[0]), kv(cache_k_win[0]), kv(cache_v_win[0]),
        state_conv_a[0], state_conv_ffn[0], bias_s, weights)

    heads = lambda a: a.reshape(1, a.shape[0], a.shape[1], N_KV_HEADS, HEAD_DIM)
    meta_b = lambda a: jnp.broadcast_to(a.reshape(1, 1, N_META, N_KV_HEADS, HEAD_DIM),
                                        (1, bsz, N_META, N_KV_HEADS, HEAD_DIM))
    return (y_p, y_s, meta_b(k_m), meta_b(v_m), heads(kwin_p), heads(vwin_p), conva_p[None], ffn_p[None],
            heads(kwin_s), heads(vwin_s), conva_s[None], ffn_s[None])
```

```python
import functools
import math

import numpy as np
import jax
import jax.numpy as jnp
from jax import lax
from jax.experimental import pallas as pl
from jax.experimental.pallas import tpu as pltpu

F32 = jnp.float32
BF16 = jnp.bfloat16

D_MODEL = 1024
CHUNK = 64
N_META = 16
D_CONV = 512
CONV_WIDTH = 31
N_HEADS = 8
N_KV_HEADS = 2
HEAD_DIM = 64
HEADS_PER_KV = N_HEADS // N_KV_HEADS
D_ATTN = N_HEADS * HEAD_DIM
D_KV = N_KV_HEADS * HEAD_DIM
WINDOW = 128
PAST_LEN = 4096
N_BUCKETS = 32
MAX_DISTANCE = 256
D_FF = 2816
FFN_CONV_WIDTH = 3
D_IN = 2 * D_CONV + D_ATTN + 2 * D_KV
EPS = 1e-6
NEG_INF = -1e30
SCALE = HEAD_DIM ** -0.5

_Q0 = 2 * D_CONV
_K0 = _Q0 + D_ATTN
_V0 = _K0 + D_KV

NKEY = 256
BAND = WINDOW + CHUNK
MAX_STACKED_ROWS = 256
HALO = 32
FFN_CW = 256
TM = 512
SB = 16
VMEM_LIMIT = 60 * 1024 * 1024


def _np_bucket(rel):
    nb = N_BUCKETS // 2
    max_exact = nb // 2
    ret = np.where(rel > 0, nb, 0)
    n = np.abs(rel)
    nf = np.maximum(n, 1).astype(np.float32)
    large = max_exact + (np.log(nf / np.float32(max_exact)) / np.float32(math.log(MAX_DISTANCE / max_exact))
                         * np.float32(nb - max_exact)).astype(np.int32)
    large = np.minimum(large, nb - 1)
    return (ret + np.where(n < max_exact, n, large)).astype(np.int32)


def _bucket_tables(past_len, dec_seq):
    i = np.arange(CHUNK)[:, None]
    j = np.arange(BAND)[None, :]
    m = np.arange(N_META)[None, :]
    band = _np_bucket(j - WINDOW - i)
    pad = np.full((CHUNK, NKEY - BAND - N_META), -1, np.int32)

    def meta_for(c):
        return _np_bucket(m - (N_META + CHUNK * c + i))

    prompt = []
    for c in range(4):
        b = np.where(c - 2 + j // CHUNK >= 0, band, -1)
        prompt.append(np.concatenate([b, meta_for(c), pad], axis=1))
    assert np.array_equal(meta_for(3), meta_for(4096))
    prompt = np.stack(prompt).astype(np.int32)

    i_s = np.arange(dec_seq)[:, None]
    band_s = np.where(j < WINDOW + dec_seq, _np_bucket(j - WINDOW - i_s), -1)
    meta_s = _np_bucket(m - (N_META + past_len + i_s))
    sample = np.concatenate([band_s, meta_s, pad[:dec_seq]], axis=1)[None].astype(np.int32)

    i_m = np.arange(N_META)[:, None]
    meta_self = np.concatenate([np.full((N_META, BAND), -1, np.int32), _np_bucket(m - i_m), pad[:N_META]],
                               axis=1)[None].astype(np.int32)
    return prompt, sample, meta_self


def _bias_kernel(table_ref, bp_ref, bs_ref, bm_ref, op_ref, os_ref, om_ref):
    for b_ref, o_ref in ((bp_ref, op_ref), (bs_ref, os_ref), (bm_ref, om_ref)):
        nv, r, _ = b_ref.shape
        for v in range(nv):
            bk = b_ref[v]
            for h in range(N_HEADS):
                acc = jnp.full((r, NKEY), NEG_INF, F32)
                for b in range(N_BUCKETS):
                    acc = jnp.where(bk == b, table_ref[b, h], acc)
                o_ref[v, r * h:r * (h + 1), :] = acc


def _build_bias(table, bp, bs, bm):
    shapes = [jax.ShapeDtypeStruct((b.shape[0], N_HEADS * b.shape[1], NKEY), F32) for b in (bp, bs, bm)]
    vm = pl.BlockSpec(memory_space=pltpu.VMEM)
    return pl.pallas_call(
        _bias_kernel,
        out_shape=shapes,
        in_specs=[pl.BlockSpec(memory_space=pltpu.SMEM), vm, vm, vm],
        out_specs=[vm, vm, vm],
        name="rel_bias",
    )(table, bp, bs, bm)


def _dot(a, b):
    return jnp.dot(a, b, preferred_element_type=F32)


def _dot_nt(a, b):
    return lax.dot_general(a, b, (((1,), (1,)), ((), ())), preferred_element_type=F32)


def _rms(x, g):
    ms = jnp.mean(x * x, axis=-1, keepdims=True)
    return (x * lax.rsqrt(ms + EPS)) * g


def _swap_halves(x):
    return pltpu.roll(x, HEAD_DIM, axis=1)


def _attention_items(nblk, r, get_q, get_k, get_v, get_bias, sinks_ref, put_out):
    hp = min(N_HEADS, max(HEADS_PER_KV, MAX_STACKED_ROWS // r))

    def block(b):
        q = get_q(b) * SCALE
        kb = get_k(b)
        vb = get_v(b)
        lo = lax.broadcasted_iota(jnp.int32, (r, 2 * HEAD_DIM), 1) < HEAD_DIM
        hi = jnp.logical_not(lo)
        bias = get_bias(b)
        for h0 in range(0, N_HEADS, hp):
            parts = []
            for h in range(h0, h0 + hp):
                g = h // HEADS_PER_KV
                qh = q[:, 128 * (h // 2):128 * (h // 2) + 128]
                if h % 2 != g:
                    qh = _swap_halves(qh)
                parts.append(jnp.where(hi if g else lo, qh, 0.0))
            qs = jnp.concatenate(parts, axis=0).astype(BF16)
            s = _dot_nt(qs, kb) + bias[r * h0:r * (h0 + hp), :]
            probs = []
            for j in range(hp):
                sink = sinks_ref[h0 + j]
                seg = s[r * j:r * (j + 1), :]
                mx = jnp.maximum(jnp.max(seg, axis=-1, keepdims=True), sink)
                e = jnp.exp(seg - mx)
                den = jnp.sum(e, axis=-1, keepdims=True) + jnp.exp(sink - mx)
                probs.append((e * (1.0 / den)).astype(BF16))
            o = _dot(jnp.concatenate(probs, axis=0), vb)
            for j in range(0, hp, 2):
                g = (h0 + j) // HEADS_PER_KV
                even = o[r * j:r * (j + 1)]
                odd = o[r * (j + 1):r * (j + 2)]
                if g == 1:
                    even = _swap_halves(even)
                else:
                    odd = _swap_halves(odd)
                put_out(b, (h0 + j) // 2, jnp.where(lo, even, odd))

    return [functools.partial(block, b) for b in range(nblk)]


def _put_heads(mix_ref, row0, r):
    def put(b, p, o):
        base = D_CONV + 128 * p
        mix_ref[row0(b):row0(b) + r, base:base + 128] = o.astype(BF16)
    return put


def _in_proj(x_ref, gmix_ref, w_in_ref, hb_ref, z_ref, ns, l, rc):
    g = gmix_ref[...]
    for s in range(ns):
        for r0 in range(0, l, rc):
            hb_ref[s * l + r0:s * l + r0 + rc, :] = _rms(x_ref[s, r0:r0 + rc, :], g).astype(BF16)
    z_ref[:, 0:_Q0] = _dot(hb_ref[...], w_in_ref[:, 0:_Q0])
    z_ref[:, _Q0:D_IN] = _dot(hb_ref[...], w_in_ref[:, _Q0:D_IN])


def _delay(x):
    return pltpu.roll(x, 1, axis=0)


def _conv31_block(glu_buf, u0, rc, cw_ref, l0, n):
    nb = (rc + 8) // 8
    acc = None
    for r in range(7, -1, -1):
        s = None
        for a in range(4):
            d = 8 * a + r
            if d >= CONV_WIDTH:
                continue
            blk = glu_buf[u0 - 8 - 8 * a:u0 + rc - 8 * a, l0:l0 + n].reshape(nb, 8, n)
            term = blk * cw_ref[d, :, l0:l0 + n][None]
            s = term if s is None else s + term
        s = s.reshape(rc + 8, n)
        acc = s if acc is None else s + _delay(acc)
    return acc[8:]


def _run(items):
    for item in items:
        item()


def _interleave(a, b):
    out, ia, ib = [], 0, 0
    while ia < len(a) or ib < len(b):
        if ib >= len(b) or (ia < len(a) and ia * len(b) <= ib * len(a)):
            out.append(a[ia])
            ia += 1
        else:
            out.append(b[ib])
            ib += 1
    return out


def _conv_items(z_ref, row0, l, glu_buf, cw_ref, dw_b_ref, ln_g_ref, ln_b_ref, mix_ref, rc):
    rn = min(rc, 64)

    def glu(r0):
        a = z_ref[row0 + r0:row0 + r0 + rc, 0:D_CONV]
        b = z_ref[row0 + r0:row0 + r0 + rc, D_CONV:2 * D_CONV]
        glu_buf[HALO + r0:HALO + r0 + rc, :] = a * jax.nn.sigmoid(b)

    def conv(r0):
        for l0 in range(0, D_CONV, 128):
            z_ref[row0 + r0:row0 + r0 + rc, l0:l0 + 128] = _conv31_block(glu_buf, HALO + r0, rc, cw_ref, l0, 128)

    def norm(r0):
        for q0 in range(r0, r0 + rc, rn):
            acc = z_ref[row0 + q0:row0 + q0 + rn, 0:D_CONV] + dw_b_ref[...]
            mu = jnp.mean(acc, axis=-1, keepdims=True)
            d = acc - mu
            var = jnp.mean(d * d, axis=-1, keepdims=True)
            yn = (d * lax.rsqrt(var + EPS)) * ln_g_ref[...] + ln_b_ref[...]
            mix_ref[row0 + q0:row0 + q0 + rn, 0:D_CONV] = (yn * jax.nn.sigmoid(yn)).astype(BF16)

    items = []
    for r0 in range(0, l, rc):
        items += [functools.partial(glu, r0), functools.partial(conv, r0), functools.partial(norm, r0)]
    return items


def _seq_pieces(m0, mc, l):
    pc = min(l, mc)
    return [(p0,) + divmod(m0 + p0, l) + (pc,) for p0 in range(0, mc, pc)]


def _out_proj_items(x_ref, mix_ref, w_out_ref, gffn_ref, x1_ref, hb_ref, ns, l, mc):
    def chunk(m0):
        acc = _dot(mix_ref[m0:m0 + mc, :], w_out_ref[...])
        for p0, s, r0, pc in _seq_pieces(m0, mc, l):
            x1 = x_ref[s, r0:r0 + pc, :] + acc[p0:p0 + pc]
            x1_ref[m0 + p0:m0 + p0 + pc, :] = x1
            hb_ref[m0 + p0:m0 + p0 + pc, :] = _rms(x1, gffn_ref[...]).astype(BF16)

    return [functools.partial(chunk, m0) for m0 in range(0, ns * l, mc)]


def _ffn_items(hb_ref, w_up_ref, fw_ref, fb_ref, gbuf, ubuf, y_ref, put_halo, save_carry, ns, l, rc):
    nch = D_FF // FFN_CW
    nb = (rc + 8) // 8

    def matmuls(ci):
        c = ci * FFN_CW
        gb = gbuf.at[ci % 2]
        ug = _dot(hb_ref[...], w_up_ref[:, c:c + FFN_CW])
        for s in range(ns):
            put_halo(gb, s, slice(c, c + FFN_CW))
            gb[s, 8:8 + l, :] = ug[s * l:(s + 1) * l]
        ubuf[ci % 2] = _dot(hb_ref[...], w_up_ref[:, D_FF + c:D_FF + c + FFN_CW])

    def elementwise(ci):
        c = ci * FFN_CW
        gb = gbuf.at[ci % 2]
        ub = ubuf.at[ci % 2]
        cols = slice(c, c + FFN_CW)
        w0 = fw_ref[0, :, cols][None]
        w1 = fw_ref[1, :, cols][None]
        w2 = fw_ref[2, :, cols][None]
        bb = fb_ref[:, cols][None]
        for s in range(ns):
            for r0 in range(0, l, rc):
                g = gb[s, r0:r0 + rc + 8, :].reshape(nb, 8, FFN_CW)
                t = (g * w0).reshape(rc + 8, FFN_CW)
                t = (g * w1).reshape(rc + 8, FFN_CW) + _delay(t)
                t = (g * w2 + bb).reshape(rc + 8, FFN_CW) + _delay(t)
                cv = t[8:]
                y = (cv * jax.nn.sigmoid(cv)) * ub[s * l + r0:s * l + r0 + rc, :]
                y_ref[s * l + r0:s * l + r0 + rc, cols] = y.astype(BF16)
            save_carry(gb, s, cols)

    return ([functools.partial(matmuls, ci) for ci in range(nch)],
            [functools.partial(elementwise, ci) for ci in range(nch)])


def _ffn_pipelined(mm, el):
    items = [mm[0]]
    for ci in range(len(el)):
        if ci + 1 < len(mm):
            items.append(mm[ci + 1])
        items.append(el[ci])
    return items


def _ffn_out_items(x1_ref, y_ref, w_down_ref, gfin_ref, o_ref, ns, l, mc):
    def chunk(m0):
        acc = _dot(y_ref[m0:m0 + mc, :], w_down_ref[...])
        for p0, s, r0, pc in _seq_pieces(m0, mc, l):
            x2 = x1_ref[m0 + p0:m0 + p0 + pc, :] + acc[p0:p0 + pc]
            o_ref[s, r0:r0 + pc, :] = _rms(x2, gfin_ref[...])

    return [functools.partial(chunk, m0) for m0 in range(0, ns * l, mc)]


def _meta_kernel(sinks_ref, x_ref, bias_ref, gmix_ref, w_in_ref, cw_ref, dw_b_ref, ln_g_ref, ln_b_ref,
                 w_out_ref, gffn_ref, w_upg_ref,
                 glu_o, k_o, v_o, gate_o,
                 hb_ref, z_ref, glu_buf, mix_ref):
    l = N_META
    _in_proj(x_ref, gmix_ref, w_in_ref, hb_ref, z_ref, 1, l, l)
    glu_buf[0:HALO, :] = jnp.zeros((HALO, D_CONV), F32)
    _run(_conv_items(z_ref, 0, l, glu_buf, cw_ref, dw_b_ref, ln_g_ref, ln_b_ref, mix_ref, l))
    glu_o[...] = glu_buf[HALO:HALO + l, :]
    k = z_ref[:, _K0:_K0 + D_KV]
    v = z_ref[:, _V0:_V0 + D_KV]
    k_o[...] = k
    v_o[...] = v
    zb = jnp.zeros((BAND, D_KV), BF16)
    zp = jnp.zeros((NKEY - BAND - N_META, D_KV), BF16)
    kb = jnp.concatenate([zb, k.astype(BF16), zp], axis=0)
    vb = jnp.concatenate([zb, v.astype(BF16), zp], axis=0)
    _run(_attention_items(1, l, lambda b: z_ref[:, _Q0:_Q0 + D_ATTN], lambda b: kb, lambda b: vb,
                          lambda b: bias_ref[0], sinks_ref, _put_heads(mix_ref, lambda b: 0, l)))
    x1 = x_ref[0] + _dot(mix_ref[...], w_out_ref[...])
    h2 = _rms(x1, gffn_ref[...]).astype(BF16)
    gate = _dot(h2, w_upg_ref[...])
    gate_o[...] = gate[N_META - 8:N_META, :]


def _run_meta(sinks, meta_tokens, bias_m, gmix, w_in, cw, dw_b, ln_g, ln_b, w_out, gffn, w_up):
    l = N_META
    full = lambda a: pl.BlockSpec(a.shape, lambda i: (0,) * a.ndim)
    x = meta_tokens[None]
    ins = [x, bias_m, gmix, w_in, cw, dw_b, ln_g, ln_b, w_out, gffn]
    in_specs = [pl.BlockSpec(memory_space=pltpu.SMEM)] + [full(a) for a in ins]
    in_specs.append(pl.BlockSpec((D_MODEL, D_FF), lambda i: (0, 0)))
    out_shape = [jax.ShapeDtypeStruct((l, D_CONV), F32), jax.ShapeDtypeStruct((l, D_KV), F32),
                 jax.ShapeDtypeStruct((l, D_KV), F32), jax.ShapeDtypeStruct((8, D_FF), F32)]
    return pl.pallas_call(
        _meta_kernel,
        grid=(1,),
        out_shape=out_shape,
        in_specs=in_specs,
        out_specs=[pl.BlockSpec(s.shape, lambda i: (0, 0)) for s in out_shape],
        scratch_shapes=[pltpu.VMEM((l, D_MODEL), BF16), pltpu.VMEM((l, D_IN), F32),
                        pltpu.VMEM((HALO + l, D_CONV), F32), pltpu.VMEM((l, D_MODEL), BF16)],
        compiler_params=pltpu.CompilerParams(vmem_limit_bytes=VMEM_LIMIT),
        name="meta_tokens",
    )(sinks, *ins, w_up)


def _prompt_kernel(sinks_ref, x_ref, glu_m_ref, k_m_ref, v_m_ref, gate_m_ref, bias_ref,
                   gmix_ref, w_in_ref, cw_ref, dw_b_ref, ln_g_ref, ln_b_ref, w_out_ref, gffn_ref,
                   w_up_ref, fw_ref, fb_ref, w_down_ref, gfin_ref,
                   y_o, kwin_o, vwin_o, conva_o, ffn_o,
                   hb_ref, z_ref, glu_buf, kv_buf, mix_ref, x1_ref, gbuf, ubuf, yh_ref, gcarry):
    t = pl.program_id(1)
    l = TM
    rc = 128
    nrc = l // rc

    @pl.when(t == 0)
    def _():
        glu_buf[0:HALO - N_META, :] = jnp.zeros((HALO - N_META, D_CONV), F32)
        glu_buf[HALO - N_META:HALO, :] = glu_m_ref[...]
        kv_buf[:, 0:WINDOW, :] = jnp.zeros((2, WINDOW, D_KV), BF16)
        gcarry[...] = gate_m_ref[...]

    def in_rows(j):
        r0 = j * rc
        hb_ref[r0:r0 + rc, :] = _rms(x_ref[0, r0:r0 + rc, :], gmix_ref[...]).astype(BF16)
        z_ref[r0:r0 + rc, 0:_Q0] = _dot(hb_ref[r0:r0 + rc, :], w_in_ref[:, 0:_Q0])
        z_ref[r0:r0 + rc, _Q0:D_IN] = _dot(hb_ref[r0:r0 + rc, :], w_in_ref[:, _Q0:D_IN])
        for idx, c0 in enumerate((_K0, _V0)):
            kv_buf[idx, WINDOW + r0:WINDOW + r0 + rc, :] = z_ref[r0:r0 + rc, c0:c0 + D_KV].astype(BF16)

    zp = jnp.zeros((NKEY - BAND - N_META, D_KV), BF16)
    km = jnp.concatenate([k_m_ref[...].astype(BF16), zp], axis=0)
    vm = jnp.concatenate([v_m_ref[...].astype(BF16), zp], axis=0)

    def variant(c):
        return jnp.where(t == 0, c, 3) if c < 3 else 3

    rows = lambda c: c * CHUNK
    attn_items = _attention_items(
        l // CHUNK, CHUNK,
        lambda c: z_ref[rows(c):rows(c) + CHUNK, _Q0:_Q0 + D_ATTN],
        lambda c: jnp.concatenate([kv_buf[0, rows(c):rows(c) + BAND, :], km], axis=0),
        lambda c: jnp.concatenate([kv_buf[1, rows(c):rows(c) + BAND, :], vm], axis=0),
        lambda c: bias_ref[variant(c)],
        sinks_ref, _put_heads(mix_ref, rows, CHUNK))

    conv_items = _conv_items(z_ref, 0, l, glu_buf, cw_ref, dw_b_ref, ln_g_ref, ln_b_ref, mix_ref, rc)
    per = len(conv_items) // nrc
    in_rows(0)
    for j in range(nrc):
        if j + 1 < nrc:
            in_rows(j + 1)
        _run(conv_items[per * j:per * (j + 1)])
    _run(attn_items)

    for win_o, c0 in ((kwin_o, _K0), (vwin_o, _V0)):
        win_o[0] = z_ref[l - WINDOW:l, c0:c0 + D_KV]
    conva_o[0] = glu_buf[HALO + l - (CONV_WIDTH - 1):HALO + l, :]
    glu_buf[0:HALO, :] = glu_buf[l:l + HALO, :]
    kv_buf[:, 0:WINDOW, :] = kv_buf[:, l:l + WINDOW, :]
    _run(_out_proj_items(x_ref, mix_ref, w_out_ref, gffn_ref, x1_ref, hb_ref, 1, l, 256))

    def put_halo(gb, s, cols):
        gb[s, 0:8, :] = gcarry[:, cols]

    def save_carry(gb, s, cols):
        gcarry[:, cols] = gb[s, l:l + 8, :]

    _run(_ffn_pipelined(*_ffn_items(hb_ref, w_up_ref, fw_ref, fb_ref, gbuf, ubuf, yh_ref, put_halo, save_carry,
                                    1, l, rc)))
    ffn_o[0] = gcarry[8 - (FFN_CONV_WIDTH - 1):8, :]
    _run(_ffn_out_items(x1_ref, yh_ref, w_down_ref, gfin_ref, y_o, 1, l, 256))


def _const_spec(a):
    return pl.BlockSpec(a.shape, lambda *_: (0,) * a.ndim, pipeline_mode=pl.Buffered(1))


def _run_prompt(sinks, x, glu_m, k_m, v_m, gate_m, bias_p, weights):
    bsz, seq, _ = x.shape
    assert seq % TM == 0 and TM % CHUNK == 0 and TM >= WINDOW
    consts = [glu_m, k_m, v_m, gate_m, bias_p, *weights]
    in_specs = ([pl.BlockSpec(memory_space=pltpu.SMEM), pl.BlockSpec((1, TM, D_MODEL), lambda b, t: (b, t, 0))]
                + [_const_spec(a) for a in consts])
    out_shape = [jax.ShapeDtypeStruct((bsz, seq, D_MODEL), F32),
                 jax.ShapeDtypeStruct((bsz, WINDOW, D_KV), F32), jax.ShapeDtypeStruct((bsz, WINDOW, D_KV), F32),
                 jax.ShapeDtypeStruct((bsz, CONV_WIDTH - 1, D_CONV), F32),
                 jax.ShapeDtypeStruct((bsz, FFN_CONV_WIDTH - 1, D_FF), F32)]
    out_specs = [pl.BlockSpec((1, TM, D_MODEL), lambda b, t: (b, t, 0))] + [
        pl.BlockSpec((1,) + s.shape[1:], lambda b, t: (b, 0, 0)) for s in out_shape[1:]]
    scratch = [pltpu.VMEM((TM, D_MODEL), BF16), pltpu.VMEM((TM, D_IN), F32),
               pltpu.VMEM((HALO + TM, D_CONV), F32), pltpu.VMEM((2, WINDOW + TM, D_KV), BF16),
               pltpu.VMEM((TM, D_MODEL), BF16), pltpu.VMEM((TM, D_MODEL), F32),
               pltpu.VMEM((2, 1, 8 + TM, FFN_CW), F32), pltpu.VMEM((2, TM, FFN_CW), F32),
               pltpu.VMEM((TM, D_FF), BF16), pltpu.VMEM((8, D_FF), F32)]
    return pl.pallas_call(
        _prompt_kernel,
        grid=(bsz, seq // TM),
        out_shape=out_shape,
        in_specs=in_specs,
        out_specs=out_specs,
        scratch_shapes=scratch,
        compiler_params=pltpu.CompilerParams(dimension_semantics=("arbitrary", "arbitrary"),
                                             vmem_limit_bytes=VMEM_LIMIT),
        name="prompt_step",
    )(sinks, x, *consts)


def _sample_kernel(sinks_ref, x_ref, kmeta_ref, vmeta_ref, kwin_ref, vwin_ref, conva_ref, ffn_ref, bias_ref,
                   gmix_ref, w_in_ref, cw_ref, dw_b_ref, ln_g_ref, ln_b_ref, w_out_ref, gffn_ref,
                   w_up_ref, fw_ref, fb_ref, w_down_ref, gfin_ref,
                   y_o, kwin_o, vwin_o, conva_o, ffn_o,
                   hb_ref, z_ref, glu_buf, mix_ref, x1_ref, gbuf, ubuf, yh_ref):
    ns, l, _ = x_ref.shape
    _in_proj(x_ref, gmix_ref, w_in_ref, hb_ref, z_ref, ns, l, l)
    glu_buf[0:HALO - 30, :] = jnp.zeros((HALO - 30, D_CONV), F32)
    zgap = jnp.zeros((BAND - WINDOW - l, D_KV), BF16)
    zp = jnp.zeros((NKEY - BAND - N_META, D_KV), BF16)
    def load_state(s):
        glu_buf[HALO - (CONV_WIDTH - 1):HALO, :] = conva_ref[s]

    def save_state(s):
        conva_o[s] = glu_buf[HALO + l - (CONV_WIDTH - 1):HALO + l, :]
        for win_ref, win_o, c0 in ((kwin_ref, kwin_o, _K0), (vwin_ref, vwin_o, _V0)):
            win_o[s, 0:WINDOW - l, :] = win_ref[s, l:WINDOW, :]
            win_o[s, WINDOW - l:WINDOW, :] = z_ref[s * l:(s + 1) * l, c0:c0 + D_KV]

    conv_items = []
    for s in range(ns):
        conv_items += ([functools.partial(load_state, s)]
                       + _conv_items(z_ref, s * l, l, glu_buf, cw_ref, dw_b_ref, ln_g_ref, ln_b_ref, mix_ref, l)
                       + [functools.partial(save_state, s)])

    def key_slots(win_ref, meta_ref, c0):
        def get(s):
            new = z_ref[s * l:(s + 1) * l, c0:c0 + D_KV]
            return jnp.concatenate([win_ref[s].astype(BF16), new.astype(BF16), zgap,
                                    meta_ref[s].astype(BF16), zp], axis=0)
        return get

    rows = lambda s: s * l
    attn_items = _attention_items(ns, l, lambda s: z_ref[rows(s):rows(s) + l, _Q0:_Q0 + D_ATTN],
                                  key_slots(kwin_ref, kmeta_ref, _K0), key_slots(vwin_ref, vmeta_ref, _V0),
                                  lambda s: bias_ref[0], sinks_ref, _put_heads(mix_ref, rows, l))
    _run(_interleave(conv_items, attn_items))

    _run(_out_proj_items(x_ref, mix_ref, w_out_ref, gffn_ref, x1_ref, hb_ref, ns, l, 256))

    nprev = FFN_CONV_WIDTH - 1
    for i in range(2):
        for s in range(ns):
            gbuf[i, s, 0:8, :] = jnp.zeros((8, FFN_CW), F32)

    def put_halo(gb, s, cols):
        gb[s, 8 - nprev:8, :] = ffn_ref[s, :, cols]

    def save_carry(gb, s, cols):
        ffn_o[s, :, cols] = gb[s, 8 + l - nprev:8 + l, :]

    _run(_ffn_pipelined(*_ffn_items(hb_ref, w_up_ref, fw_ref, fb_ref, gbuf, ubuf, yh_ref, put_halo, save_carry,
                                    ns, l, l)))
    _run(_ffn_out_items(x1_ref, yh_ref, w_down_ref, gfin_ref, y_o, ns, l, 256))


def _run_sample(sinks, x, kmeta, vmeta, kwin, vwin, conva, ffn, bias_s, weights):
    nstream, l, _ = x.shape
    assert nstream % SB == 0 and l % 16 == 0 and l <= CHUNK and kwin.shape[1] == WINDOW
    rows = SB * l
    per_stream = [x, kmeta, vmeta, kwin, vwin, conva, ffn]
    consts = [bias_s, *weights]
    in_specs = ([pl.BlockSpec(memory_space=pltpu.SMEM)]
                + [pl.BlockSpec((SB,) + a.shape[1:], lambda i: (i, 0, 0)) for a in per_stream]
                + [_const_spec(a) for a in consts])
    out_shape = [jax.ShapeDtypeStruct(a.shape, F32) for a in (x, kwin, vwin, conva, ffn)]
    out_specs = [pl.BlockSpec((SB,) + s.shape[1:], lambda i: (i, 0, 0)) for s in out_shape]
    scratch = [pltpu.VMEM((rows, D_MODEL), BF16), pltpu.VMEM((rows, D_IN), F32),
               pltpu.VMEM((HALO + l, D_CONV), F32),
               pltpu.VMEM((rows, D_MODEL), BF16), pltpu.VMEM((rows, D_MODEL), F32),
               pltpu.VMEM((2, SB, 8 + l, FFN_CW), F32), pltpu.VMEM((2, rows, FFN_CW), F32),
               pltpu.VMEM((rows, D_FF), BF16)]
    return pl.pallas_call(
        _sample_kernel,
        grid=(nstream // SB,),
        out_shape=out_shape,
        in_specs=in_specs,
        out_specs=out_specs,
        scratch_shapes=scratch,
        compiler_params=pltpu.CompilerParams(dimension_semantics=("arbitrary",), vmem_limit_bytes=VMEM_LIMIT),
        name="sample_step",
    )(sinks, *per_stream, *consts)


def kernel(x_prompt, x_sample, cache_k_meta, cache_v_meta, cache_k_win, cache_v_win, state_conv_a, state_conv_ffn,
           meta_tokens, rel_bias_table, norm_mix, w_in, conv_dw_w, conv_dw_b, conv_ln_g, conv_ln_b, attn_sinks,
           w_out, norm_ffn, w_up, ffn_dw_w, ffn_dw_b, w_down, norm_final):
    depth = w_in.shape[0]
    assert depth == 1, "single-layer step"
    bsz = x_prompt.shape[0]
    nstream, dec_seq, _ = x_sample.shape
    win = cache_k_win.shape[2]
    assert win == WINDOW

    bp, bs, bm = _bucket_tables(PAST_LEN, dec_seq)
    bias_p, bias_s, bias_m = _build_bias(rel_bias_table, jnp.asarray(bp), jnp.asarray(bs), jnp.asarray(bm))

    row = lambda a: a.reshape(1, -1)
    w_in_b = w_in[0].astype(BF16)
    w_out_b = w_out[0].astype(BF16)
    w_up_b = w_up[0].astype(BF16)
    w_down_b = w_down[0].astype(BF16)
    sinks = attn_sinks[0]
    gmix, gffn, gfin = row(norm_mix[0]), row(norm_ffn[0]), row(norm_final)
    cw = jnp.broadcast_to(conv_dw_w[0][::-1][:, None, :], (CONV_WIDTH, 8, D_CONV))
    dw_b = row(conv_dw_b[0])
    ln_g, ln_b = row(conv_ln_g[0]), row(conv_ln_b[0])
    fw = jnp.broadcast_to(ffn_dw_w[0][:, None, :], (FFN_CONV_WIDTH, 8, D_FF))
    fb = jnp.broadcast_to(ffn_dw_b[0][None, :], (8, D_FF))

    glu_m, k_m, v_m, gate_m = _run_meta(sinks, meta_tokens, bias_m, gmix, w_in_b, cw, dw_b, ln_g, ln_b,
                                        w_out_b, gffn, w_up_b)

    weights = [gmix, w_in_b, cw, dw_b, ln_g, ln_b, w_out_b, gffn, w_up_b, fw, fb, w_down_b, gfin]
    y_p, kwin_p, vwin_p, conva_p, ffn_p = _run_prompt(sinks, x_prompt, glu_m, k_m, v_m, gate_m, bias_p, weights)

    kv = lambda a: a.reshape(a.shape[0], a.shape[1], D_KV)
    y_s, kwin_s, vwin_s, conva_s, ffn_s = _run_sample(
        sinks, x_sample, kv(cache_k_meta[0]), kv(cache_v_meta[0]), kv(cache_k_win[0]), kv(cache_v_win[0]),
        state_conv_a[0], state_conv_ffn[0], bias_s, weights)

    heads = lambda a: a.reshape(1, a.shape[0], a.shape[1], N_KV_HEADS, HEAD_DIM)
    meta_b = lambda a: jnp.broadcast_to(a.reshape(1, 1, N_META, N_KV_HEADS, HEAD_DIM),
                                        (1, bsz, N_META, N_KV_HEADS, HEAD_DIM))
    return (y_p, y_s, meta_b(k_m), meta_b(v_m), heads(kwin_p), heads(vwin_p), conva_p[None], ffn_p[None],
            heads(kwin_s), heads(vwin_s), conva_s[None], ffn_s[None])
```
